```python
import jax, jax.numpy as jnp
from jax import lax
import numpy as np

D_MODEL = 1024
BATCH = 2
SEQ = 8192
DEPTH = 4
DEC_BATCH = 32
DEC_SEQ = 4
PAST_LEN = 8192
PAGE_SIZE = 128

HEAD_DIM = 64
H_FOX = 6
H_RET = 5
H_SB = 5
W_FOX = H_FOX * HEAD_DIM
W_RET = H_RET * HEAD_DIM
W_SB = H_SB * HEAD_DIM
MIX_W = W_FOX + W_RET + W_SB
PROJ_W = 3 * W_FOX + H_FOX + 4 * W_RET + 3 * W_SB
D_FF = 2816
Q_BLOCK = 128
RET_CHUNK = 128
ROPE_BASE = 10000.0
EPS = 1e-6
N_NORMS = 6

kernel_name = 'hybrid_fox_retention_stickbreak_step'


def rmsnorm(x, g):
    xf = x.astype(jnp.float32)
    y = xf * lax.rsqrt(jnp.mean(xf * xf, axis=-1, keepdims=True) + EPS)
    return (y * g.astype(jnp.float32)).astype(x.dtype)


def swiglu(x, w_in, w_out):
    gate, up = jnp.split(x @ w_in, 2, axis=-1)
    return (jax.nn.silu(gate) * up) @ w_out


def split_proj(p):
    sizes = (W_FOX, W_FOX, W_FOX, H_FOX, W_RET, W_RET, W_RET, W_RET, W_SB, W_SB, W_SB)
    qa, ka, va, fa, qr, kr, vr, gr, qs, ks, vs = jnp.split(p, np.cumsum(sizes)[:-1].tolist(), axis=-1)
    heads = lambda a, h: a.reshape(a.shape[:-1] + (h, HEAD_DIM))
    return (heads(qa, H_FOX), heads(ka, H_FOX), heads(va, H_FOX), fa,
            heads(qr, H_RET), heads(kr, H_RET), heads(vr, H_RET), gr,
            heads(qs, H_SB), heads(ks, H_SB), heads(vs, H_SB))


def rope(x, pos):
    half = HEAD_DIM // 2
    inv_freq = ROPE_BASE ** (-jnp.arange(half, dtype=jnp.float32) / half)
    ang = pos.astype(jnp.float32)[:, None] * inv_freq[None, :]
    cos = jnp.cos(ang)[None, :, None, :]
    sin = jnp.sin(ang)[None, :, None, :]
    xf = x.astype(jnp.float32)
    x1, x2 = xf[..., :half], xf[..., half:]
    return jnp.concatenate([x1 * cos - x2 * sin, x1 * sin + x2 * cos], axis=-1).astype(x.dtype)


def gather_pages(pool, page_table):
    g = pool[page_table]
    return g.reshape((g.shape[0], g.shape[1] * g.shape[2]) + g.shape[3:])


def fox_attend(q, k, v, cq, ck, qpos, kpos):
    s = jnp.einsum('bqhd,bkhd->bhqk', q, k).astype(jnp.float32) * (HEAD_DIM ** -0.5)
    s = s + (jnp.swapaxes(cq, 1, 2)[:, :, :, None] - jnp.swapaxes(ck, 1, 2)[:, :, None, :])
    mask = kpos[None, :] <= qpos[:, None]
    p = jax.nn.softmax(jnp.where(mask, s, -jnp.inf), axis=-1)
    return jnp.einsum('bhqk,bkhd->bqhd', p.astype(v.dtype), v)


def sb_attend(q, k, v, qpos, kpos):
    z = jnp.einsum('bqhd,bkhd->bhqk', q, k).astype(jnp.float32) * (HEAD_DIM ** -0.5)
    mask = kpos[None, :] < qpos[:, None]
    log_rest = jnp.where(mask, jax.nn.log_sigmoid(-z), 0.0)
    after = lax.cumsum(log_rest, axis=3, reverse=True) - log_rest
    a = jnp.where(mask, jnp.exp(jax.nn.log_sigmoid(z) + after), 0.0)
    return jnp.einsum('bhqk,bkhd->bqhd', a.astype(v.dtype), v)


def sweep_query_blocks(attend, per_query):
    T = per_query[0].shape[1]
    nb = T // Q_BLOCK
    blocks = tuple(a.reshape((a.shape[0], nb, Q_BLOCK) + a.shape[2:]).swapaxes(0, 1) for a in per_query)
    starts = jnp.arange(nb, dtype=jnp.int32) * Q_BLOCK
    offs = jnp.arange(Q_BLOCK, dtype=jnp.int32)
    out = lax.map(lambda args: attend(args[0] + offs, *args[1]), (starts, blocks))
    out = out.swapaxes(0, 1)
    return out.reshape((out.shape[0], T) + out.shape[3:])


def ret_log_gamma():
    return jnp.log(1.0 - 2.0 ** (-5.0 - jnp.arange(H_RET, dtype=jnp.float32)))


def retention_chunk(S, q, k, v):
    q, k, v = q.astype(jnp.float32), k.astype(jnp.float32), v.astype(jnp.float32)
    L = q.shape[1]
    lg = ret_log_gamma()
    i = jnp.arange(L)
    diff = i[:, None] - i[None, :]
    decay = jnp.where(diff[None] >= 0, jnp.exp(jnp.maximum(diff, 0)[None].astype(jnp.float32) * lg[:, None, None]), 0.0)
    scores = jnp.einsum('blhd,bmhd->bhlm', q, k) * decay[None]
    inner = jnp.einsum('bhlm,bmhe->blhe', scores, v)
    q_dec = jnp.exp((i + 1).astype(jnp.float32)[:, None] * lg[None, :])
    cross = jnp.einsum('blhd,bhde->blhe', q * q_dec[None, :, :, None], S)
    k_dec = jnp.exp((L - 1 - i).astype(jnp.float32)[:, None] * lg[None, :])
    S_new = jnp.exp(L * lg)[None, :, None, None] * S + jnp.einsum('blhd,blhe->bhde', k * k_dec[None, :, :, None], v)
    return S_new, inner + cross


def retention_prompt(q, k, v):
    B, T, H, d = q.shape
    nc = T // RET_CHUNK
    to_chunks = lambda a: a.reshape(B, nc, RET_CHUNK, H, d).swapaxes(0, 1)
    S0 = jnp.zeros((B, H, d, d), jnp.float32)
    S_fin, outs = lax.scan(lambda S, c: retention_chunk(S, c[0], c[1], c[2]), S0,
                           (to_chunks(q), to_chunks(k), to_chunks(v)))
    return S_fin, outs.swapaxes(0, 1).reshape(B, T, H, d)


def merge_groups(o_fox, o_ret, g_ret, o_sb, gn_g):
    B, L = o_fox.shape[:2]
    mu = jnp.mean(o_ret, axis=-1, keepdims=True)
    var = jnp.mean(jnp.square(o_ret - mu), axis=-1, keepdims=True)
    r = ((o_ret - mu) * lax.rsqrt(var + EPS)).reshape(B, L, W_RET) * gn_g.astype(jnp.float32)
    r = (r * jax.nn.silu(g_ret.astype(jnp.float32))).astype(o_fox.dtype)
    return jnp.concatenate([o_fox.reshape(B, L, W_FOX), r, o_sb.reshape(B, L, W_SB)], axis=-1)


def mixers_prompt(u, w_in, b_f, gn_g):
    T = u.shape[1]
    qa, ka, va, fa, qr, kr, vr, gr, qs, ks, vs = split_proj(u @ w_in)
    pos = jnp.arange(T, dtype=jnp.int32)
    logf = jax.nn.log_sigmoid(fa.astype(jnp.float32) + b_f.astype(jnp.float32))
    c = jnp.cumsum(logf, axis=1)
    o_fox = sweep_query_blocks(lambda qp, qb, cb: fox_attend(qb, ka, va, cb, c, qp, pos), (qa, c))
    S_fin, o_ret = retention_prompt(rope(qr, pos), rope(kr, pos) * (HEAD_DIM ** -0.5), vr)
    o_sb = sweep_query_blocks(lambda qp, qb: sb_attend(qb, ks, vs, qp, pos), (qs,))
    return merge_groups(o_fox, o_ret, gr, o_sb, gn_g), (ka, va, logf, ks, vs, S_fin)


def mixers_sample(u, w_in, b_f, gn_g, ck_fox, cv_fox, clf_fox, ck_sb, cv_sb, S, page_table):
    L = u.shape[1]
    past = page_table.shape[1] * ck_fox.shape[1]
    qa, ka, va, fa, qr, kr, vr, gr, qs, ks, vs = split_proj(u @ w_in)
    qpos = past + jnp.arange(L, dtype=jnp.int32)
    kpos = jnp.arange(past + L, dtype=jnp.int32)
    logf = jax.nn.log_sigmoid(fa.astype(jnp.float32) + b_f.astype(jnp.float32))
    k_fox = jnp.concatenate([gather_pages(ck_fox, page_table), ka], axis=1)
    v_fox = jnp.concatenate([gather_pages(cv_fox, page_table), va], axis=1)
    c_all = jnp.cumsum(jnp.concatenate([gather_pages(clf_fox, page_table).astype(jnp.float32), logf], axis=1), axis=1)
    o_fox = fox_attend(qa, k_fox, v_fox, c_all[:, past:], c_all, qpos, kpos)
    S_new, o_ret = retention_chunk(S.astype(jnp.float32), rope(qr, qpos), rope(kr, qpos) * (HEAD_DIM ** -0.5), vr)
    k_sb = jnp.concatenate([gather_pages(ck_sb, page_table), ks], axis=1)
    v_sb = jnp.concatenate([gather_pages(cv_sb, page_table), vs], axis=1)
    o_sb = sb_attend(qs, k_sb, v_sb, qpos, kpos)
    return merge_groups(o_fox, o_ret, gr, o_sb, gn_g), (ka, va, logf, ks, vs, S_new)


def layer(x, mix_fn, g, w1_in, w1_out, w_out, w2_in, w2_out):
    h = x + 0.5 * rmsnorm(swiglu(rmsnorm(x, g[0]), w1_in, w1_out), g[1])
    m, st = mix_fn(rmsnorm(h, g[2]))
    h = h + rmsnorm(m @ w_out, g[3])
    h = h + 0.5 * rmsnorm(swiglu(rmsnorm(h, g[4]), w2_in, w2_out), g[5])
    return h, st


def setup_inputs(seed: int = 0) -> dict:
    key = jax.random.key(seed)
    ks = jax.random.split(key, 20)
    n_pages = PAST_LEN // PAGE_SIZE
    n_pool = (5 * DEC_BATCH * n_pages + 3) // 4
    nrm = lambda k, s, sc: jax.random.normal(k, s, jnp.float32) * sc
    page_table = jax.random.permutation(ks[0], n_pool)[:DEC_BATCH * n_pages].reshape(DEC_BATCH, n_pages).astype(jnp.int32)
    return {
        'x_prompt': nrm(ks[1], (BATCH, SEQ, D_MODEL), 1.0),
        'x_sample': nrm(ks[2], (DEC_BATCH, DEC_SEQ, D_MODEL), 1.0),
        'cache_fox_k': nrm(ks[3], (DEPTH, n_pool, PAGE_SIZE, H_FOX, HEAD_DIM), 1.0),
        'cache_fox_v': nrm(ks[4], (DEPTH, n_pool, PAGE_SIZE, H_FOX, HEAD_DIM), 1.0),
        'cache_fox_logf': jax.nn.log_sigmoid(2.5 + nrm(ks[5], (DEPTH, n_pool, PAGE_SIZE, H_FOX), 1.0)),
        'cache_sb_k': nrm(ks[6], (DEPTH, n_pool, PAGE_SIZE, H_SB, HEAD_DIM), 1.0),
        'cache_sb_v': nrm(ks[7], (DEPTH, n_pool, PAGE_SIZE, H_SB, HEAD_DIM), 1.0),
        'state_ret': nrm(ks[8], (DEPTH, DEC_BATCH, H_RET, HEAD_DIM, HEAD_DIM), 1.0),
        'page_table': page_table,
        'norm_g': 1.0 + nrm(ks[9], (DEPTH, N_NORMS, D_MODEL), 0.02),
        'w_ffn1_in': nrm(ks[10], (DEPTH, D_MODEL, 2 * D_FF), D_MODEL ** -0.5),
        'w_ffn1_out': nrm(ks[11], (DEPTH, D_FF, D_MODEL), D_FF ** -0.5),
        'w_mix_in': nrm(ks[12], (DEPTH, D_MODEL, PROJ_W), D_MODEL ** -0.5),
        'b_forget': jax.random.uniform(ks[13], (DEPTH, H_FOX), jnp.float32, 1.0, 4.0),
        'ret_gn_g': 1.0 + nrm(ks[14], (DEPTH, W_RET), 0.02),
        'w_mix_out': nrm(ks[15], (DEPTH, MIX_W, D_MODEL), MIX_W ** -0.5),
        'w_ffn2_in': nrm(ks[16], (DEPTH, D_MODEL, 2 * D_FF), D_MODEL ** -0.5),
        'w_ffn2_out': nrm(ks[17], (DEPTH, D_FF, D_MODEL), D_FF ** -0.5),
    }


def reference(x_prompt, x_sample, cache_fox_k, cache_fox_v, cache_fox_logf, cache_sb_k, cache_sb_v,
              state_ret, page_table, norm_g, w_ffn1_in, w_ffn1_out, w_mix_in, b_forget, ret_gn_g,
              w_mix_out, w_ffn2_in, w_ffn2_out):
    yp, ys = x_prompt, x_sample
    st_p, st_s = [], []
    for l in range(DEPTH):
        yp, sp = layer(yp, lambda u: mixers_prompt(u, w_mix_in[l], b_forget[l], ret_gn_g[l]),
                       norm_g[l], w_ffn1_in[l], w_ffn1_out[l], w_mix_out[l], w_ffn2_in[l], w_ffn2_out[l])
        ys, ss = layer(ys, lambda u: mixers_sample(u, w_mix_in[l], b_forget[l], ret_gn_g[l],
                                                   cache_fox_k[l], cache_fox_v[l], cache_fox_logf[l],
                                                   cache_sb_k[l], cache_sb_v[l], state_ret[l], page_table),
                       norm_g[l], w_ffn1_in[l], w_ffn1_out[l], w_mix_out[l], w_ffn2_in[l], w_ffn2_out[l])
        st_p.append(sp)
        st_s.append(ss)
    stk = lambda sts, i: jnp.stack([s[i] for s in sts], axis=0)
    return (yp, ys,
            stk(st_p, 0), stk(st_p, 1), stk(st_p, 2), stk(st_p, 3), stk(st_p, 4), stk(st_p, 5),
            stk(st_s, 0), stk(st_s, 1), stk(st_s, 2), stk(st_s, 3), stk(st_s, 4), stk(st_s, 5))
```

```python
import functools

import numpy as np
import jax
import jax.numpy as jnp
from jax import lax
from jax.experimental import pallas as pl
from jax.experimental.pallas import tpu as pltpu

HEAD_DIM = 64
H_FOX = 6
H_RET = 5
H_SB = 5
GROUP_W = 384
HEADS_PER_BLOCK = 2
EPS = 1e-6
ROPE_BASE = 10000.0
Q_SCALE = HEAD_DIM ** -0.5
VMEM_LIMIT_BYTES = 48 * 1024 * 1024
F32 = jnp.float32
BF16 = jnp.bfloat16
NEG_INF = float("-inf")


def _cparams(*sem):
    return pltpu.CompilerParams(dimension_semantics=sem, vmem_limit_bytes=VMEM_LIMIT_BYTES)


def _dot(a, b):
    return jnp.dot(a, b, preferred_element_type=F32)


def _dot_nt(a, b):
    return lax.dot_general(a, b, (((1,), (1,)), ((), ())), preferred_element_type=F32)


def _dot_tn(a, b):
    return lax.dot_general(a, b, (((0,), (0,)), ((), ())), preferred_element_type=F32)


def _split3(x):
    hi = x.astype(BF16)
    r = x - hi.astype(F32)
    mid = r.astype(BF16)
    lo = (r - mid.astype(F32)).astype(BF16)
    return hi, mid, lo


def _dot_acc3(x, m):
    hi, mid, lo = _split3(x)
    return _dot(hi, m) + _dot(mid, m) + _dot(lo, m)


def _rms(x, g):
    return x * lax.rsqrt(jnp.mean(x * x, axis=-1, keepdims=True) + EPS) * g


def _softplus_parts(z):
    lp = jnp.log1p(jnp.exp(-jnp.abs(z)))
    return -jnp.maximum(z, 0.0) - lp, jnp.minimum(z, 0.0) - lp


def _ffn_kernel(x_ref, gpre_ref, wg_ref, wu_ref, wo_ref, gpost_ref, o_ref, xn_ref, acc_ref):
    j = pl.program_id(1)

    @pl.when(j == 0)
    def _():
        xn_ref[...] = _rms(x_ref[...], gpre_ref[...]).astype(BF16)
        acc_ref[...] = jnp.zeros_like(acc_ref)

    xn = xn_ref[...]
    gate = _dot(xn, wg_ref[...])
    up = _dot(xn, wu_ref[...])
    hidden = (gate * jax.nn.sigmoid(gate) * up).astype(BF16)
    acc_ref[...] += _dot(hidden, wo_ref[...])

    @pl.when(j == pl.num_programs(1) - 1)
    def _():
        o_ref[...] = x_ref[...] + 0.5 * _rms(acc_ref[...], gpost_ref[...])


def _ffn_block(x, g_pre, w_in, w_out, g_post):
    m, d = x.shape
    dff = w_out.shape[0]
    tm = min(512, m)
    nf = 2
    tf = dff // nf
    return pl.pallas_call(
        _ffn_kernel,
        grid=(m // tm, nf),
        in_specs=[
            pl.BlockSpec((tm, d), lambda i, j: (i, 0)),
            pl.BlockSpec((1, d), lambda i, j: (0, 0)),
            pl.BlockSpec((d, tf), lambda i, j: (0, j)),
            pl.BlockSpec((d, tf), lambda i, j: (0, nf + j)),
            pl.BlockSpec((tf, d), lambda i, j: (j, 0)),
            pl.BlockSpec((1, d), lambda i, j: (0, 0)),
        ],
        out_specs=pl.BlockSpec((tm, d), lambda i, j: (i, 0)),
        out_shape=jax.ShapeDtypeStruct((m, d), F32),
        scratch_shapes=[pltpu.VMEM((tm, d), BF16), pltpu.VMEM((tm, d), F32)],
        compiler_params=_cparams("parallel", "arbitrary"),
        name="ffn_block",
    )(x, g_pre, w_in, w_in, w_out, g_post)


def _proj_prompt_kernel(x_ref, g_ref, wrow_ref, wt_ref, bf_ref, cos_ref, sin_ref, tri_ref,
                        qa_ref, qs_ref, qr_ref, kr_ref, vr_ref, gr_ref,
                        kat_ref, vat_ref, kst_ref, vst_ref,
                        katb_ref, vatb_ref, kstb_ref, vstb_ref,
                        lf_ref, c_ref, carry_ref):
    w = GROUP_W
    t = pl.program_id(1)

    @pl.when(t == 0)
    def _():
        carry_ref[...] = jnp.zeros_like(carry_ref)

    xn = _rms(x_ref[0], g_ref[...]).astype(BF16)
    pr = _dot(xn, wrow_ref[...])
    cos = cos_ref[...]
    sin = sin_ref[...]
    qa_ref[0] = (pr[:, 0:w] * Q_SCALE).astype(BF16)
    qs_ref[0] = (pr[:, w:2 * w] * Q_SCALE).astype(BF16)
    qr_ref[0] = (pr[:, 2 * w:3 * w] * cos + pr[:, 3 * w:4 * w] * sin).astype(BF16)
    kr_ref[0] = ((pr[:, 4 * w:5 * w] * cos + pr[:, 5 * w:6 * w] * sin) * Q_SCALE).astype(BF16)
    vr_ref[0] = pr[:, 6 * w:7 * w].astype(BF16)
    gr_ref[0] = pr[:, 7 * w:8 * w]

    pt = _dot_nt(wt_ref[...], xn)
    kat = pt[0:w]
    vat = pt[w:2 * w]
    kst = pt[2 * w:3 * w]
    vst = pt[3 * w:4 * w]
    kat_ref[0] = kat
    vat_ref[0] = vat
    kst_ref[0] = kst[0:H_SB * HEAD_DIM]
    vst_ref[0] = vst[0:H_SB * HEAD_DIM]
    katb_ref[0] = kat.astype(BF16)
    vatb_ref[0] = vat.astype(BF16)
    kstb_ref[0] = kst.astype(BF16)
    vstb_ref[0] = vst.astype(BF16)

    logf = jax.nn.log_sigmoid(pt[4 * w:4 * w + 16] + bf_ref[...])
    lf_ref[0] = logf
    c = _dot_acc3(logf, tri_ref[...]) + carry_ref[...]
    c_ref[0] = c
    carry_ref[...] = c[:, -1:]


def _proj_prompt(x, g, wrow, wt, bf, cos, sin):
    b, t, d = x.shape
    tm = min(512, t)
    w = GROUP_W
    tri = (jnp.arange(tm)[:, None] <= jnp.arange(tm)[None, :]).astype(BF16)
    row = lambda dt: jax.ShapeDtypeStruct((b, t, w), dt)
    col = lambda n, dt: jax.ShapeDtypeStruct((b, n, t), dt)
    rspec = pl.BlockSpec((1, tm, w), lambda bi, ti: (bi, ti, 0))
    cspec = lambda n: pl.BlockSpec((1, n, tm), lambda bi, ti: (bi, 0, ti))
    const = lambda a: pl.BlockSpec(a.shape, lambda bi, ti: (0,) * a.ndim)
    nsb = H_SB * HEAD_DIM
    return pl.pallas_call(
        _proj_prompt_kernel,
        grid=(b, t // tm),
        in_specs=[
            pl.BlockSpec((1, tm, d), lambda bi, ti: (bi, ti, 0)),
            const(g), const(wrow), const(wt), const(bf),
            pl.BlockSpec((tm, w), lambda bi, ti: (ti, 0)),
            pl.BlockSpec((tm, w), lambda bi, ti: (ti, 0)),
            const(tri),
        ],
        out_specs=[rspec] * 6 + [cspec(w), cspec(w), cspec(nsb), cspec(nsb)] + [cspec(w)] * 4
        + [cspec(16), cspec(16)],
        out_shape=[row(BF16)] * 5 + [row(F32)]
        + [col(w, F32), col(w, F32), col(nsb, F32), col(nsb, F32)] + [col(w, BF16)] * 4
        + [col(16, F32), col(16, F32)],
        scratch_shapes=[pltpu.VMEM((16, 1), F32)],
        compiler_params=_cparams("parallel", "arbitrary"),
        name="proj_prompt",
    )(x, g, wrow, wt, bf, cos, sin, tri)


def _proj_sample_kernel(x_ref, g_ref, w_ref, bf_ref, cos_ref, sin_ref,
                        qa_ref, qs_ref, qr_ref, kr_ref, vr_ref, gr_ref,
                        ka_ref, va_ref, ks_ref, vs_ref, lf_ref):
    w = GROUP_W
    xn = _rms(x_ref[...], g_ref[...]).astype(BF16)
    pr = _dot(xn, w_ref[...])
    cos = cos_ref[...]
    sin = sin_ref[...]
    qa_ref[...] = pr[:, 0:w] * Q_SCALE
    qs_ref[...] = pr[:, w:2 * w] * Q_SCALE
    qr_ref[...] = pr[:, 2 * w:3 * w] * cos + pr[:, 3 * w:4 * w] * sin
    kr_ref[...] = (pr[:, 4 * w:5 * w] * cos + pr[:, 5 * w:6 * w] * sin) * Q_SCALE
    vr_ref[...] = pr[:, 6 * w:7 * w]
    gr_ref[...] = pr[:, 7 * w:8 * w]
    ka_ref[...] = pr[:, 8 * w:9 * w]
    va_ref[...] = pr[:, 9 * w:10 * w]
    ks_ref[...] = pr[:, 10 * w:11 * w]
    vs_ref[...] = pr[:, 11 * w:12 * w]
    lf_ref[...] = jax.nn.log_sigmoid(pr[:, 12 * w:12 * w + 128] + bf_ref[...])


def _proj_sample(x, g, w_all, bf_row, cos, sin):
    m = x.shape[0]
    w = GROUP_W
    return pl.pallas_call(
        _proj_sample_kernel,
        out_shape=[jax.ShapeDtypeStruct((m, w), F32)] * 10 + [jax.ShapeDtypeStruct((m, 128), F32)],
        compiler_params=pltpu.CompilerParams(vmem_limit_bytes=VMEM_LIMIT_BYTES),
        name="proj_sample",
    )(x, g, w_all, bf_row, cos, sin)


def _merge_kernel(h_ref, of_ref, oret_ref, gr_ref, osb_ref, gn_ref, avg_ref,
                  w1_ref, w2_ref, w3_ref, g_ref, o_ref):
    x = oret_ref[...]
    avg = avg_ref[...]
    hi, mid, lo = _split3(x)
    mu = _dot(hi, avg) + _dot(mid, avg) + _dot(lo, avg)
    dev = x - mu
    var = _dot_acc3(dev * dev, avg)
    gate = gr_ref[...]
    r = dev * lax.rsqrt(var + EPS) * gn_ref[...] * (gate * jax.nn.sigmoid(gate))
    y = _dot(of_ref[...], w1_ref[...]) + _dot(r.astype(BF16), w2_ref[...]) + _dot(osb_ref[...], w3_ref[...])
    o_ref[...] = h_ref[...] + _rms(y, g_ref[...])


def _merge_out(h, o_fox, o_ret, gr, o_sb, gn, avg, w1, w2, w3, g):
    m, d = h.shape
    w = GROUP_W
    tm = min(512, m)
    rows = lambda n: pl.BlockSpec((tm, n), lambda i: (i, 0))
    const = lambda a: pl.BlockSpec(a.shape, lambda i: (0,) * a.ndim)
    return pl.pallas_call(
        _merge_kernel,
        grid=(m // tm,),
        in_specs=[rows(d), rows(w), rows(w), rows(w), rows(w), const(gn), const(avg),
                  const(w1), const(w2), const(w3), const(g)],
        out_specs=rows(d),
        out_shape=jax.ShapeDtypeStruct((m, d), F32),
        compiler_params=_cparams("parallel"),
        name="merge_out",
    )(h, o_fox, o_ret, gr, o_sb, gn, avg, w1, w2, w3, g)


def _head_lane_mask(shape, hh):
    lane = lax.broadcasted_iota(jnp.int32, shape, len(shape) - 1)
    return (lane >= HEAD_DIM * hh) & (lane < HEAD_DIM * (hh + 1))


def _fox_prompt_kernel(q_ref, kt_ref, vt_ref, ccol_ref, crow_ref, o_ref, *, tq):
    i = pl.program_id(2)
    q2 = q_ref[0]
    row = lax.broadcasted_iota(jnp.int32, (tq, tq), 0)
    colp = lax.broadcasted_iota(jnp.int32, (tq, tq), 1)
    outs = []
    for hh in range(HEADS_PER_BLOCK):
        qh = jnp.where(_head_lane_mask(q2.shape, hh), q2, jnp.zeros_like(q2))
        cq = ccol_ref[0, 0][:, hh:hh + 1]

        def step(j, carry, masked, qh=qh, cq=cq, hh=hh):
            m, l, acc = carry
            off = pl.multiple_of(j * tq, tq)
            k = kt_ref[0, :, pl.ds(off, tq)]
            v = vt_ref[0, :, pl.ds(off, tq)]
            ck = crow_ref[0, 0, hh:hh + 1, pl.ds(off, tq)]
            s = _dot(qh, k) + (cq - ck)
            if masked:
                s = jnp.where(colp <= row, s, NEG_INF)
            m_new = jnp.maximum(m, jnp.max(s, axis=1, keepdims=True))
            alpha = jnp.exp(m - m_new)
            p = jnp.exp(s - m_new)
            l = alpha * l + jnp.sum(p, axis=1, keepdims=True)
            acc = alpha * acc + _dot_nt(p.astype(BF16), v)
            return m_new, l, acc

        init = (jnp.full((tq, 1), NEG_INF, F32), jnp.zeros((tq, 1), F32), jnp.zeros((tq, 128), F32))
        carry = lax.fori_loop(0, i, functools.partial(step, masked=False), init)
        _, l, acc = step(i, carry, True)
        outs.append(acc / l)
    o_ref[0] = jnp.where(_head_lane_mask(outs[0].shape, 0), outs[0], outs[1]).astype(o_ref.dtype)


def _fox_prompt(q, kt, vt, ccol, crow):
    b, t, w = q.shape
    tq = min(256, t)
    nb = w // 128
    return pl.pallas_call(
        functools.partial(_fox_prompt_kernel, tq=tq),
        grid=(b, nb, t // tq),
        in_specs=[
            pl.BlockSpec((1, tq, 128), lambda bi, p, i: (bi, i, p)),
            pl.BlockSpec((1, 128, t), lambda bi, p, i: (bi, p, 0)),
            pl.BlockSpec((1, 128, t), lambda bi, p, i: (bi, p, 0)),
            pl.BlockSpec((1, 1, tq, HEADS_PER_BLOCK), lambda bi, p, i: (bi, p, i, 0)),
            pl.BlockSpec((1, 1, 8, t), lambda bi, p, i: (bi, p, 0, 0)),
        ],
        out_specs=pl.BlockSpec((1, tq, 128), lambda bi, p, i: (bi, i, p)),
        out_shape=jax.ShapeDtypeStruct((b, t, w), BF16),
        compiler_params=_cparams("parallel", "parallel", "arbitrary"),
        name="fox_prompt",
    )(q, kt, vt, ccol, crow)


def _sb_prompt_kernel(q_ref, kt_ref, vt_ref, u_ref, o_ref, *, tq):
    i = pl.program_id(2)
    q2 = q_ref[0]
    u = u_ref[...]
    row = lax.broadcasted_iota(jnp.int32, (tq, tq), 0)
    colp = lax.broadcasted_iota(jnp.int32, (tq, tq), 1)
    outs = []
    for hh in range(HEADS_PER_BLOCK):
        qh = jnp.where(_head_lane_mask(q2.shape, hh), q2, jnp.zeros_like(q2))

        def step(j, carry, masked, qh=qh):
            rest, acc = carry
            off = pl.multiple_of(j * tq, tq)
            k = kt_ref[0, :, pl.ds(off, tq)]
            v = vt_ref[0, :, pl.ds(off, tq)]
            z = _dot(qh, k)
            log_rest, log_take = _softplus_parts(z)
            if masked:
                log_rest = jnp.where(colp < row, log_rest, 0.0)
            hi = log_rest.astype(BF16)
            lo = (log_rest - hi.astype(F32)).astype(BF16)
            after = _dot(hi, u) + _dot(lo, u) + rest
            a = jnp.exp(log_take + after)
            if masked:
                a = jnp.where(colp < row, a, 0.0)
            acc = acc + _dot_nt(a.astype(BF16), v)
            rest = rest + jnp.sum(log_rest, axis=1, keepdims=True)
            return rest, acc

        carry = step(i, (jnp.zeros((tq, 1), F32), jnp.zeros((tq, 128), F32)), True)
        _, acc = lax.fori_loop(0, i, lambda n, c: step(i - 1 - n, c, False), carry)
        outs.append(acc)
    o_ref[0] = jnp.where(_head_lane_mask(outs[0].shape, 0), outs[0], outs[1]).astype(o_ref.dtype)


def _sb_prompt(q, kt, vt):
    b, t, w = q.shape
    tq = min(256, t)
    nb = w // 128
    u = (jnp.arange(tq)[:, None] > jnp.arange(tq)[None, :]).astype(BF16)
    return pl.pallas_call(
        functools.partial(_sb_prompt_kernel, tq=tq),
        grid=(b, nb, t // tq),
        in_specs=[
            pl.BlockSpec((1, tq, 128), lambda bi, p, i: (bi, i, p)),
            pl.BlockSpec((1, 128, t), lambda bi, p, i: (bi, p, 0)),
            pl.BlockSpec((1, 128, t), lambda bi, p, i: (bi, p, 0)),
            pl.BlockSpec((tq, tq), lambda bi, p, i: (0, 0)),
        ],
        out_specs=pl.BlockSpec((1, tq, 128), lambda bi, p, i: (bi, i, p)),
        out_shape=jax.ShapeDtypeStruct((b, t, w), BF16),
        compiler_params=_cparams("parallel", "parallel", "arbitrary"),
        name="sb_prompt",
    )(q, kt, vt, u)


def _log_gamma(head):
    pw = jnp.zeros(head.shape, F32)
    for h in range(HEADS_PER_BLOCK * (GROUP_W // 128)):
        pw = jnp.where(head == h, 2.0 ** (-5.0 - h), pw)
    return jnp.log(1.0 - pw)


def _ret_prompt_kernel(q_ref, k_ref, v_ref, o_ref, s_ref, state_ref, dec_ref, qkd_ref, *, ch):
    p = pl.program_id(1)
    c = pl.program_id(2)

    @pl.when(c == 0)
    def _():
        state_ref[...] = jnp.zeros_like(state_ref)
        ii = lax.broadcasted_iota(jnp.int32, (ch, ch), 0)
        jj = lax.broadcasted_iota(jnp.int32, (ch, ch), 1)
        diff = ii - jj
        pos = lax.broadcasted_iota(jnp.int32, (ch, 1), 0).astype(F32)
        for hh in range(HEADS_PER_BLOCK):
            lg = _log_gamma(jnp.full((1, 1), HEADS_PER_BLOCK * p + hh, jnp.int32))
            dec_ref[hh] = jnp.where(diff >= 0, jnp.exp(jnp.maximum(diff, 0).astype(F32) * lg), 0.0)
            qkd_ref[hh, :, 0:1] = jnp.exp((pos + 1.0) * lg)
            qkd_ref[hh, :, 1:2] = jnp.exp((ch - 1.0 - pos) * lg)
            qkd_ref[hh, :, 2:3] = jnp.broadcast_to(jnp.exp(ch * lg), (ch, 1))

    q2 = q_ref[0]
    k2 = k_ref[0]
    v2 = v_ref[0]
    outs = []
    for hh in range(HEADS_PER_BLOCK):
        qh = jnp.where(_head_lane_mask(q2.shape, hh), q2, jnp.zeros_like(q2))
        scores = _dot_nt(qh, k2) * dec_ref[hh]
        inner = _dot(scores.astype(BF16), v2)
        state = state_ref[hh]
        q_dec = (qh.astype(F32) * qkd_ref[hh, :, 0:1]).astype(BF16)
        cross = _dot(q_dec, state.astype(BF16))
        outs.append(inner + cross)
        k_dec = (k2.astype(F32) * qkd_ref[hh, :, 1:2]).astype(BF16)
        state_ref[hh] = qkd_ref[hh, 0:1, 2:3] * state + _dot_tn(k_dec, v2)
    o_ref[0] = jnp.where(_head_lane_mask(outs[0].shape, 0), outs[0], outs[1])

    @pl.when(c == pl.num_programs(2) - 1)
    def _():
        s_ref[0, 0] = state_ref[...]


def _ret_prompt(q, k, v):
    b, t, w = q.shape
    ch = min(256, t)
    nb = w // 128
    tile = pl.BlockSpec((1, ch, 128), lambda bi, p, c: (bi, c, p))
    return pl.pallas_call(
        functools.partial(_ret_prompt_kernel, ch=ch),
        grid=(b, nb, t // ch),
        in_specs=[tile, tile, tile],
        out_specs=[tile, pl.BlockSpec((1, 1, HEADS_PER_BLOCK, 128, 128), lambda bi, p, c: (bi, p, 0, 0, 0))],
        out_shape=[jax.ShapeDtypeStruct((b, t, w), F32),
                   jax.ShapeDtypeStruct((b, nb, HEADS_PER_BLOCK, 128, 128), F32)],
        scratch_shapes=[pltpu.VMEM((HEADS_PER_BLOCK, 128, 128), F32),
                        pltpu.VMEM((HEADS_PER_BLOCK, ch, ch), F32),
                        pltpu.VMEM((HEADS_PER_BLOCK, ch, 3), F32)],
        compiler_params=_cparams("parallel", "parallel", "arbitrary"),
        name="ret_prompt",
    )(q, k, v)


def _ret_sample_kernel(q_ref, k_ref, v_ref, s_ref, o_ref, snew_ref, *, n_new):
    q = q_ref[0]
    k = k_ref[0]
    v = v_ref[0]
    ii = lax.broadcasted_iota(jnp.int32, (n_new, n_new), 0)
    jj = lax.broadcasted_iota(jnp.int32, (n_new, n_new), 1)
    diff = ii - jj
    pos = lax.broadcasted_iota(jnp.int32, (n_new, 1), 0).astype(F32)
    o_ref[0] = jnp.zeros(o_ref.shape[1:], F32)
    for h in range(H_RET):
        lg = _log_gamma(jnp.full((1, 1), h, jnp.int32))
        sl = slice(h * HEAD_DIM, (h + 1) * HEAD_DIM)
        qh = q[:, sl]
        kh = k[:, sl]
        vh = v[:, sl].astype(BF16)
        decay = jnp.where(diff >= 0, jnp.exp(jnp.maximum(diff, 0).astype(F32) * lg), 0.0)
        scores = _dot_nt(qh.astype(BF16), kh.astype(BF16)) * decay
        inner = _dot(scores.astype(BF16), vh)
        state = s_ref[0, h]
        cross = _dot((qh * jnp.exp((pos + 1.0) * lg)).astype(BF16), state.astype(BF16))
        o_ref[0, :, sl] = inner + cross
        k_dec = (kh * jnp.exp((n_new - 1.0 - pos) * lg)).astype(BF16)
        snew_ref[0, h] = jnp.exp(n_new * lg) * state + _dot_tn(k_dec, vh)


def _ret_sample(q, k, v, state):
    nb, n_new, w = q.shape
    tok = pl.BlockSpec((1, n_new, w), lambda i: (i, 0, 0))
    st = pl.BlockSpec((1, H_RET, HEAD_DIM, HEAD_DIM), lambda i: (i, 0, 0, 0))
    return pl.pallas_call(
        functools.partial(_ret_sample_kernel, n_new=n_new),
        grid=(nb,),
        in_specs=[tok, tok, tok, st],
        out_specs=[tok, st],
        out_shape=[jax.ShapeDtypeStruct((nb, n_new, w), F32),
                   jax.ShapeDtypeStruct(state.shape, F32)],
        compiler_params=_cparams("parallel"),
        name="ret_sample",
    )(q, k, v, state)


ROWS_PER_TOKEN = 8


def _row_token(shape):
    return lax.broadcasted_iota(jnp.int32, shape, 0) // ROWS_PER_TOKEN


def _collapse_heads(acc, n_new, width):
    r = lax.broadcasted_iota(jnp.int32, acc.shape, 0) % ROWS_PER_TOKEN
    lane_head = lax.broadcasted_iota(jnp.int32, acc.shape, 1) // HEAD_DIM
    kept = jnp.where(r == lane_head, acc, 0.0)
    return jnp.sum(kept.reshape(n_new, ROWS_PER_TOKEN, width), axis=1)


def _fox_decode_kernel(pt_ref, qbd_ref, knew_ref, vnew_ref, lfcol_ref, lfmat_ref, lfc_ref, u_ref, *rest,
                       n_pp, n_new, n_pages):
    k_refs = rest[:n_pp]
    v_refs = rest[n_pp:2 * n_pp]
    o_ref = rest[2 * n_pp]
    m_ref, l_ref, acc_ref, suf_ref, lf8_ref = rest[2 * n_pp + 1:]
    b = pl.program_id(0)
    g = pl.program_id(1)
    nrow = n_new * ROWS_PER_TOKEN
    qbd = qbd_ref[0]
    u = u_ref[...]
    lane = lax.broadcasted_iota(jnp.int32, (nrow, 128), 1)

    lfcol = lfcol_ref[0]
    cnew_col = jnp.concatenate(
        [sum(lfcol[j * ROWS_PER_TOKEN:(j + 1) * ROWS_PER_TOKEN] for j in range(t + 1)) for t in range(n_new)], axis=0)

    def attend(s, v, valid):
        if valid is not None:
            s = jnp.where(valid, s, NEG_INF)
        m_new = jnp.maximum(m_ref[...], jnp.max(s, axis=1, keepdims=True))
        alpha = jnp.exp(m_ref[...] - m_new)
        p = jnp.exp(s - m_new)
        l_ref[...] = alpha * l_ref[...] + jnp.sum(p, axis=1, keepdims=True)
        acc_ref[...] = alpha * acc_ref[...] + v(p.astype(BF16))
        m_ref[...] = m_new

    @pl.when(g == 0)
    def _():
        m_ref[...] = jnp.full_like(m_ref, NEG_INF)
        l_ref[...] = jnp.zeros_like(l_ref)
        acc_ref[...] = jnp.zeros_like(acc_ref)
        suf_ref[...] = jnp.zeros_like(suf_ref)
        lf8_ref[...] = jnp.zeros_like(lf8_ref)
        lfmat = lfmat_ref[0]
        cnew_mat = jnp.zeros_like(lfmat)
        for j in range(n_new):
            cnew_mat = cnew_mat + jnp.where(lane >= j, lfmat[:, j:j + 1], 0.0)
        s = _dot(qbd, knew_ref[0]) + (cnew_col - cnew_mat)
        valid = lane <= _row_token((nrow, 128))
        attend(s, lambda p: _dot(p, vnew_ref[0]), valid)

    for ii in range(n_pp):
        page = pt_ref[b, n_pages - 1 - (g * n_pp + ii)]
        for h in range(H_FOX):
            lf8_ref[h:h + 1, :] = lfc_ref[0, h, pl.ds(page, 1), :]
        lf8 = lf8_ref[...]
        suffix = _dot_acc3(lf8, u) + suf_ref[...]
        bias = jnp.concatenate([suffix] * n_new, axis=0) + cnew_col
        kp = k_refs[ii][0, 0].reshape(H_FOX * HEAD_DIM, 128).astype(BF16)
        vp = v_refs[ii][0, 0].reshape(H_FOX * HEAD_DIM, 128).astype(BF16)
        s = _dot(qbd, kp) + bias
        attend(s, lambda p, vp=vp: _dot_nt(p, vp), None)
        suf_ref[...] = suf_ref[...] + jnp.sum(lf8, axis=1, keepdims=True)

    @pl.when(g == pl.num_programs(1) - 1)
    def _():
        o_ref[0] = _collapse_heads(acc_ref[...] / l_ref[...], n_new, GROUP_W)


def _pages_per_step(n_pages):
    return min(8, n_pages)


def _fox_decode(layer, page_table, qbd, knew_t, vnew, lfcol, lfmat, cache_k, cache_v, cache_lf):
    nb, nrow, w = qbd.shape
    n_new = nrow // ROWS_PER_TOKEN
    n_pages = page_table.shape[1]
    n_pp = _pages_per_step(n_pages)
    u = (jnp.arange(128)[:, None] > jnp.arange(128)[None, :]).astype(BF16)
    per_seq = lambda a: pl.BlockSpec((1,) + a.shape[1:], lambda b, g, pt: (b,) + (0,) * (a.ndim - 1))

    def page_spec(ii):
        return pl.BlockSpec((1, 1) + cache_k.shape[2:],
                            lambda b, g, pt: (layer, pt[b, n_pages - 1 - (g * n_pp + ii)], 0, 0, 0))

    grid_spec = pltpu.PrefetchScalarGridSpec(
        num_scalar_prefetch=1,
        grid=(nb, n_pages // n_pp),
        in_specs=[per_seq(qbd), per_seq(knew_t), per_seq(vnew), per_seq(lfcol), per_seq(lfmat),
                  pl.BlockSpec((1,) + cache_lf.shape[1:], lambda b, g, pt: (layer, 0, 0, 0)),
                  pl.BlockSpec(u.shape, lambda b, g, pt: (0, 0))]
        + [page_spec(ii) for ii in range(n_pp)] * 2,
        out_specs=pl.BlockSpec((1, n_new, w), lambda b, g, pt: (b, 0, 0)),
        scratch_shapes=[pltpu.VMEM((nrow, 1), F32), pltpu.VMEM((nrow, 1), F32), pltpu.VMEM((nrow, w), F32),
                        pltpu.VMEM((ROWS_PER_TOKEN, 1), F32), pltpu.VMEM((ROWS_PER_TOKEN, 128), F32)],
    )
    return pl.pallas_call(
        functools.partial(_fox_decode_kernel, n_pp=n_pp, n_new=n_new, n_pages=n_pages),
        grid_spec=grid_spec,
        out_shape=jax.ShapeDtypeStruct((nb, n_new, w), F32),
        compiler_params=_cparams("parallel", "arbitrary"),
        name="fox_decode",
    )(page_table, qbd, knew_t, vnew, lfcol, lfmat, cache_lf, u, *([cache_k] * n_pp), *([cache_v] * n_pp))


def _sb_decode_kernel(pt_ref, qbd_ref, knew_ref, vnew_ref, u_ref, *rest, n_pp, n_new):
    k_refs = rest[:n_pp]
    v_refs = rest[n_pp:2 * n_pp]
    o_ref = rest[2 * n_pp]
    rest_ref, acc_ref = rest[2 * n_pp + 1:]
    g = pl.program_id(1)
    nrow = n_new * ROWS_PER_TOKEN
    width = H_SB * HEAD_DIM
    qbd = qbd_ref[0]
    u = u_ref[...]

    def attend(z, v, valid):
        log_rest, log_take = _softplus_parts(z)
        if valid is not None:
            log_rest = jnp.where(valid, log_rest, 0.0)
        hi = log_rest.astype(BF16)
        lo = (log_rest - hi.astype(F32)).astype(BF16)
        after = _dot(hi, u) + _dot(lo, u) + rest_ref[...]
        a = jnp.exp(log_take + after)
        if valid is not None:
            a = jnp.where(valid, a, 0.0)
        acc_ref[...] += v(a.astype(BF16))
        rest_ref[...] += jnp.sum(log_rest, axis=1, keepdims=True)

    @pl.when(g == 0)
    def _():
        rest_ref[...] = jnp.zeros_like(rest_ref)
        acc_ref[...] = jnp.zeros_like(acc_ref)
        lane = lax.broadcasted_iota(jnp.int32, (nrow, 128), 1)
        valid = lane < _row_token((nrow, 128))
        attend(_dot(qbd, knew_ref[0]), lambda a: _dot(a, vnew_ref[0]), valid)

    for ii in range(n_pp):
        kp = k_refs[ii][0, 0].reshape(width, 128).astype(BF16)
        vp = v_refs[ii][0, 0].reshape(width, 128).astype(BF16)
        attend(_dot(qbd, kp), lambda a, vp=vp: _dot_nt(a, vp), None)

    @pl.when(g == pl.num_programs(1) - 1)
    def _():
        o_ref[0] = _collapse_heads(acc_ref[...], n_new, width)


def _sb_decode(layer, page_table, qbd, knew_t, vnew, cache_k, cache_v):
    nb, nrow, width = qbd.shape
    n_new = nrow // ROWS_PER_TOKEN
    n_pages = page_table.shape[1]
    n_pp = _pages_per_step(n_pages)
    u = (jnp.arange(128)[:, None] > jnp.arange(128)[None, :]).astype(BF16)
    per_seq = lambda a: pl.BlockSpec((1,) + a.shape[1:], lambda b, g, pt: (b,) + (0,) * (a.ndim - 1))

    def page_spec(ii):
        return pl.BlockSpec((1, 1) + cache_k.shape[2:],
                            lambda b, g, pt: (layer, pt[b, n_pages - 1 - (g * n_pp + ii)], 0, 0, 0))

    grid_spec = pltpu.PrefetchScalarGridSpec(
        num_scalar_prefetch=1,
        grid=(nb, n_pages // n_pp),
        in_specs=[per_seq(qbd), per_seq(knew_t), per_seq(vnew), pl.BlockSpec(u.shape, lambda b, g, pt: (0, 0))]
        + [page_spec(ii) for ii in range(n_pp)] * 2,
        out_specs=pl.BlockSpec((1, n_new, width), lambda b, g, pt: (b, 0, 0)),
        scratch_shapes=[pltpu.VMEM((nrow, 1), F32), pltpu.VMEM((nrow, width), F32)],
    )
    return pl.pallas_call(
        functools.partial(_sb_decode_kernel, n_pp=n_pp, n_new=n_new),
        grid_spec=grid_spec,
        out_shape=jax.ShapeDtypeStruct((nb, n_new, width), F32),
        compiler_params=_cparams("parallel", "arbitrary"),
        name="sb_decode",
    )(page_table, qbd, knew_t, vnew, u, *([cache_k] * n_pp), *([cache_v] * n_pp))


def _pad_cols(a, width=GROUP_W):
    return jnp.pad(a, ((0, 0), (0, width - a.shape[1])))


def _split_mix_weight(w):
    wf, wr, ws = H_FOX * HEAD_DIM, H_RET * HEAD_DIM, H_SB * HEAD_DIM
    sizes = (wf, wf, wf, H_FOX, wr, wr, wr, wr, ws, ws, ws)
    offs = np.cumsum((0,) + sizes)
    return [w[:, offs[i]:offs[i + 1]] for i in range(len(sizes))]


def _rope_swap_perm(n_heads):
    idx = np.arange(n_heads * HEAD_DIM)
    return (idx // HEAD_DIM) * HEAD_DIM + (idx % HEAD_DIM + HEAD_DIM // 2) % HEAD_DIM


def _rope_tables(pos, n_heads):
    half = HEAD_DIM // 2
    inv_freq = ROPE_BASE ** (-jnp.arange(half, dtype=F32) / half)
    ang = pos.astype(F32)[:, None] * inv_freq[None, :]
    cos, sin = jnp.cos(ang), jnp.sin(ang)
    cos_t = jnp.tile(jnp.concatenate([cos, cos], axis=1), (1, n_heads))
    sin_t = jnp.tile(jnp.concatenate([-sin, sin], axis=1), (1, n_heads))
    return _pad_cols(cos_t), _pad_cols(sin_t)


def _block_diag_queries(q, n_heads):
    nb, n_new, w = q.shape
    slot = jnp.arange(ROWS_PER_TOKEN)[:, None]
    lane_head = (jnp.arange(w) // HEAD_DIM)[None, :]
    mask = (slot == lane_head) & (slot < n_heads)
    qb = jnp.where(mask[None, None], q[:, :, None, :], 0.0)
    return qb.reshape(nb, n_new * ROWS_PER_TOKEN, w).astype(BF16)


def _new_keys_t(k, width):
    nb, n_new, _ = k.shape
    kt = jnp.swapaxes(k[:, :, :width], 1, 2)
    return jnp.pad(kt, ((0, 0), (0, 0), (0, 128 - n_new))).astype(BF16)


def _new_values(v, width):
    nb, n_new, _ = v.shape
    return jnp.pad(v[:, :, :width], ((0, 0), (0, 128 - n_new), (0, 0))).astype(BF16)


def kernel(x_prompt, x_sample, cache_fox_k, cache_fox_v, cache_fox_logf, cache_sb_k, cache_sb_v, state_ret,
           page_table, norm_g, w_ffn1_in, w_ffn1_out, w_mix_in, b_forget, ret_gn_g, w_mix_out, w_ffn2_in,
           w_ffn2_out):
    depth = norm_g.shape[0]
    bsz, seq, d = x_prompt.shape
    nb, n_new, _ = x_sample.shape
    n_pages, page_size = page_table.shape[1], cache_fox_k.shape[2]
    past = n_pages * page_size
    wf, wr, ws = H_FOX * HEAD_DIM, H_RET * HEAD_DIM, H_SB * HEAD_DIM
    w = GROUP_W

    ck_fox = jnp.transpose(cache_fox_k, (0, 1, 3, 4, 2))
    cv_fox = jnp.transpose(cache_fox_v, (0, 1, 3, 4, 2))
    ck_sb = jnp.transpose(cache_sb_k, (0, 1, 3, 4, 2))
    cv_sb = jnp.transpose(cache_sb_v, (0, 1, 3, 4, 2))
    clf = jnp.transpose(cache_fox_logf, (0, 3, 1, 2))

    cos_p, sin_p = _rope_tables(jnp.arange(seq, dtype=jnp.int32), H_RET)
    cos_s, sin_s = _rope_tables(past + jnp.arange(n_new, dtype=jnp.int32), H_RET)
    cos_s, sin_s = jnp.tile(cos_s, (nb, 1)), jnp.tile(sin_s, (nb, 1))
    perm = _rope_swap_perm(H_RET)
    avg = jnp.asarray(np.kron(np.eye(w // HEAD_DIM), np.full((HEAD_DIM, HEAD_DIM), 1.0 / HEAD_DIM)), BF16)

    yp = x_prompt.reshape(bsz * seq, d)
    ys = x_sample.reshape(nb * n_new, d)
    st_p, st_s = [], []
    for l in range(depth):
        g = [norm_g[l, i][None, :] for i in range(norm_g.shape[1])]
        w1_in, w1_out = w_ffn1_in[l].astype(BF16), w_ffn1_out[l].astype(BF16)
        w2_in, w2_out = w_ffn2_in[l].astype(BF16), w_ffn2_out[l].astype(BF16)
        qa, ka, va, fa, qr, kr, vr, gr, qs, ks, vs = _split_mix_weight(w_mix_in[l])
        rows = [qa, _pad_cols(qs), _pad_cols(qr), _pad_cols(qr[:, perm]), _pad_cols(kr), _pad_cols(kr[:, perm]),
                _pad_cols(vr), _pad_cols(gr)]
        wrow = jnp.concatenate(rows, axis=1).astype(BF16)
        wt = jnp.concatenate([ka, va, _pad_cols(ks), _pad_cols(vs), _pad_cols(fa, 16)], axis=1).T.astype(BF16)
        w_samp = jnp.concatenate(rows + [ka, va, _pad_cols(ks), _pad_cols(vs), _pad_cols(fa, 128)], axis=1).astype(BF16)
        bf_col = jnp.pad(b_forget[l], (0, 16 - H_FOX))[:, None]
        bf_row = jnp.pad(b_forget[l], (0, 128 - H_FOX))[None, :]
        gn = _pad_cols(ret_gn_g[l][None, :])
        wo = w_mix_out[l]
        wo1 = wo[:wf].astype(BF16)
        wo2 = jnp.pad(wo[wf:wf + wr], ((0, w - wr), (0, 0))).astype(BF16)
        wo3 = jnp.pad(wo[wf + wr:], ((0, w - ws), (0, 0))).astype(BF16)

        hp = _ffn_block(yp, g[0], w1_in, w1_out, g[1])
        (qa_p, qs_p, qr_p, kr_p, vr_p, gr_p, kat, vat, kst, vst, katb, vatb, kstb, vstb, lft, ct) = _proj_prompt(
            hp.reshape(bsz, seq, d), g[2], wrow, wt, bf_col, cos_p, sin_p)
        c6 = ct[:, :H_FOX].reshape(bsz, w // 128, HEADS_PER_BLOCK, seq)
        ccol = jnp.transpose(c6, (0, 1, 3, 2))
        crow = jnp.pad(c6, ((0, 0), (0, 0), (0, 8 - HEADS_PER_BLOCK), (0, 0)))
        o_fox = _fox_prompt(qa_p, katb, vatb, ccol, crow)
        o_ret, s_raw = _ret_prompt(qr_p, kr_p, vr_p)
        o_sb = _sb_prompt(qs_p, kstb, vstb)
        hp = _merge_out(hp, o_fox.reshape(bsz * seq, w), o_ret.reshape(bsz * seq, w), gr_p.reshape(bsz * seq, w),
                        o_sb.reshape(bsz * seq, w), gn, avg, wo1, wo2, wo3, g[3])
        yp = _ffn_block(hp, g[4], w2_in, w2_out, g[5])
        s_fin = jnp.stack([s_raw[:, h // 2, h % 2, (h % 2) * HEAD_DIM:(h % 2 + 1) * HEAD_DIM,
                                 (h % 2) * HEAD_DIM:(h % 2 + 1) * HEAD_DIM] for h in range(H_RET)], axis=1)
        to_heads = lambda a, nh: jnp.transpose(a.reshape(bsz, nh, HEAD_DIM, seq), (0, 3, 1, 2))
        st_p.append((to_heads(kat, H_FOX), to_heads(vat, H_FOX), jnp.transpose(lft[:, :H_FOX], (0, 2, 1)),
                     to_heads(kst, H_SB), to_heads(vst, H_SB), s_fin))

        hs = _ffn_block(ys, g[0], w1_in, w1_out, g[1])
        (qa_s, qs_s, qr_s, kr_s, vr_s, gr_s, ka_s, va_s, ks_s, vs_s, lf_s) = _proj_sample(
            hs, g[2], w_samp, bf_row, cos_s, sin_s)
        seqs = lambda a: a.reshape(nb, n_new, a.shape[-1])
        lf_new = seqs(lf_s)[:, :, :ROWS_PER_TOKEN]
        lfcol = lf_new.reshape(nb, n_new * ROWS_PER_TOKEN, 1)
        lfmat = jnp.broadcast_to(jnp.transpose(lf_new, (0, 2, 1))[:, None], (nb, n_new, ROWS_PER_TOKEN, n_new))
        lfmat = jnp.pad(lfmat.reshape(nb, n_new * ROWS_PER_TOKEN, n_new), ((0, 0), (0, 0), (0, 128 - n_new)))
        o_fox_s = _fox_decode(l, page_table, _block_diag_queries(seqs(qa_s), H_FOX), _new_keys_t(seqs(ka_s), wf),
                              _new_values(seqs(va_s), wf), lfcol, lfmat, ck_fox, cv_fox, clf)
        o_sb_s = _sb_decode(l, page_table, _block_diag_queries(seqs(qs_s)[:, :, :ws], H_SB),
                            _new_keys_t(seqs(ks_s), ws), _new_values(seqs(vs_s), ws), ck_sb, cv_sb)
        o_ret_s, s_new = _ret_sample(seqs(qr_s), seqs(kr_s), seqs(vr_s), state_ret[l])
        hs = _merge_out(hs, o_fox_s.reshape(nb * n_new, w).astype(BF16), o_ret_s.reshape(nb * n_new, w), gr_s,
                        _pad_cols(o_sb_s.reshape(nb * n_new, ws)).astype(BF16), gn, avg, wo1, wo2, wo3, g[3])
        ys = _ffn_block(hs, g[4], w2_in, w2_out, g[5])
        heads = lambda a, nh: a[:, :nh * HEAD_DIM].reshape(nb, n_new, nh, HEAD_DIM)
        st_s.append((heads(ka_s, H_FOX), heads(va_s, H_FOX), seqs(lf_s)[:, :, :H_FOX],
                     heads(ks_s, H_SB), heads(vs_s, H_SB), s_new))

    stk = lambda sts, i: jnp.stack([s[i] for s in sts], axis=0)
    return (yp.reshape(bsz, seq, d), ys.reshape(nb, n_new, d),
            stk(st_p, 0), stk(st_p, 1), stk(st_p, 2), stk(st_p, 3), stk(st_p, 4), stk(st_p, 5),
            stk(st_s, 0), stk(st_s, 1), stk(st_s, 2), stk(st_s, 3), stk(st_s, 4), stk(st_s, 5))
```

```python
import functools
import math

import numpy as np
import jax
import jax.numpy as jnp
from jax import lax
from jax.experimental import pallas as pl
from jax.experimental.pallas import tpu as pltpu

HEAD_DIM = 64
H_FOX = 6
H_RET = 5
H_SB = 5
GROUP_W = 384
HEADS_PER_BLOCK = 2
EPS = 1e-6
ROPE_BASE = 10000.0
Q_SCALE = HEAD_DIM ** -0.5
LOG2E = math.log2(math.e)
AUG_ROWS = 16
V_ROWS = 80
CUM_BLOCK = 256
PROMPT_TQ = 1024
FOX_TK = 1024
SB_TK = 1024
VMEM_LIMIT_BYTES = 48 * 1024 * 1024
F32 = jnp.float32
BF16 = jnp.bfloat16
NEG_INF = float("-inf")


def _cparams(*sem):
    return pltpu.CompilerParams(dimension_semantics=sem, vmem_limit_bytes=VMEM_LIMIT_BYTES)


def _dot(a, b):
    return jnp.dot(a, b, preferred_element_type=F32)


def _dot_nt(a, b):
    return lax.dot_general(a, b, (((1,), (1,)), ((), ())), preferred_element_type=F32)


def _dot_tn(a, b):
    return lax.dot_general(a, b, (((0,), (0,)), ((), ())), preferred_element_type=F32)


def _split3(x):
    hi = x.astype(BF16)
    r = x - hi.astype(F32)
    mid = r.astype(BF16)
    lo = (r - mid.astype(F32)).astype(BF16)
    return hi, mid, lo


def _dot_acc3(x, m):
    hi, mid, lo = _split3(x)
    return _dot(hi, m) + _dot(mid, m) + _dot(lo, m)


def _rms(x, g):
    return x * lax.rsqrt(jnp.mean(x * x, axis=-1, keepdims=True) + EPS) * g


def _softplus_parts(z):
    lp = jnp.log1p(jnp.exp(-jnp.abs(z)))
    return -jnp.maximum(z, 0.0) - lp, jnp.minimum(z, 0.0) - lp


def _ffn_kernel(x_ref, gpre_ref, wg_ref, wu_ref, wo_ref, gpost_ref, o_ref, xn_ref, acc_ref):
    j = pl.program_id(1)

    @pl.when(j == 0)
    def _():
        xn_ref[...] = _rms(x_ref[...], gpre_ref[...]).astype(BF16)
        acc_ref[...] = jnp.zeros_like(acc_ref)

    xn = xn_ref[...]
    gate = _dot(xn, wg_ref[...])
    up = _dot(xn, wu_ref[...])
    hidden = (gate * jax.nn.sigmoid(gate) * up).astype(BF16)
    acc_ref[...] += _dot(hidden, wo_ref[...])

    @pl.when(j == pl.num_programs(1) - 1)
    def _():
        o_ref[...] = x_ref[...] + 0.5 * _rms(acc_ref[...], gpost_ref[...])


def _ffn_block(x, g_pre, w_in, w_out, g_post):
    m, d = x.shape
    dff = w_out.shape[0]
    tm = min(512, m)
    nf = 2
    tf = dff // nf
    return pl.pallas_call(
        _ffn_kernel,
        grid=(m // tm, nf),
        in_specs=[
            pl.BlockSpec((tm, d), lambda i, j: (i, 0)),
            pl.BlockSpec((1, d), lambda i, j: (0, 0)),
            pl.BlockSpec((d, tf), lambda i, j: (0, j)),
            pl.BlockSpec((d, tf), lambda i, j: (0, nf + j)),
            pl.BlockSpec((tf, d), lambda i, j: (j, 0)),
            pl.BlockSpec((1, d), lambda i, j: (0, 0)),
        ],
        out_specs=pl.BlockSpec((tm, d), lambda i, j: (i, 0)),
        out_shape=jax.ShapeDtypeStruct((m, d), F32),
        scratch_shapes=[pltpu.VMEM((tm, d), BF16), pltpu.VMEM((tm, d), F32)],
        compiler_params=_cparams("parallel", "arbitrary"),
        name="ffn_block",
    )(x, g_pre, w_in, w_in, w_out, g_post)


def _proj_prompt_kernel(x_ref, g_ref, wrow_ref, wt_ref, bf_ref, cos_ref, sin_ref, tri_ref,
                        kar_ref, ksr_ref, qr_ref, kr_ref, vr_ref, gr_ref,
                        kat_ref, vat_ref, kst_ref, vst_ref,
                        vatb_ref, vstb_ref, qat_ref, qst_ref,
                        lf_ref, chi_ref, cmid_ref, clo_ref, carry_ref):
    w = GROUP_W
    t = pl.program_id(1)

    @pl.when(t == 0)
    def _():
        carry_ref[...] = jnp.zeros_like(carry_ref)

    xn = _rms(x_ref[0], g_ref[...]).astype(BF16)
    pr = _dot(xn, wrow_ref[...])
    cos = cos_ref[...]
    sin = sin_ref[...]
    kar_ref[0] = pr[:, 0:w].astype(BF16)
    ksr_ref[0] = pr[:, w:2 * w].astype(BF16)
    qr_ref[0] = (pr[:, 2 * w:3 * w] * cos + pr[:, 3 * w:4 * w] * sin).astype(BF16)
    kr_ref[0] = ((pr[:, 4 * w:5 * w] * cos + pr[:, 5 * w:6 * w] * sin) * Q_SCALE).astype(BF16)
    vr_ref[0] = pr[:, 6 * w:7 * w].astype(BF16)
    gr_ref[0] = pr[:, 7 * w:8 * w]

    pt = _dot_nt(wt_ref[...], xn)
    kat = pt[0:w]
    vat = pt[w:2 * w]
    kst = pt[2 * w:3 * w]
    vst = pt[3 * w:4 * w]
    kat_ref[0] = kat
    vat_ref[0] = vat
    kst_ref[0] = kst[0:H_SB * HEAD_DIM]
    vst_ref[0] = vst[0:H_SB * HEAD_DIM]
    vatb_ref[0] = vat.astype(BF16)
    vstb_ref[0] = vst.astype(BF16)
    qat_ref[0] = (pt[4 * w:5 * w] * (Q_SCALE * LOG2E)).astype(BF16)
    qst_ref[0] = (pt[5 * w:6 * w] * (Q_SCALE * LOG2E)).astype(BF16)

    logf = jax.nn.log_sigmoid(pt[6 * w:6 * w + 16] + bf_ref[...])
    lf_ref[0] = logf
    c = _dot_acc3(logf, tri_ref[...]) + carry_ref[...]
    carry_ref[...] = c[:, -1:]
    hi, mid, lo = _split3(c * LOG2E)
    chi_ref[0] = hi
    cmid_ref[0] = mid
    clo_ref[0] = lo


def _proj_prompt(x, g, wrow, wt, bf, cos, sin):
    b, t, d = x.shape
    tm = min(512, t)
    w = GROUP_W
    tri = (jnp.arange(tm)[:, None] <= jnp.arange(tm)[None, :]).astype(BF16)
    row = lambda dt: jax.ShapeDtypeStruct((b, t, w), dt)
    col = lambda n, dt: jax.ShapeDtypeStruct((b, n, t), dt)
    rspec = pl.BlockSpec((1, tm, w), lambda bi, ti: (bi, ti, 0))
    cspec = lambda n: pl.BlockSpec((1, n, tm), lambda bi, ti: (bi, 0, ti))
    const = lambda a: pl.BlockSpec(a.shape, lambda bi, ti: (0,) * a.ndim)
    nsb = H_SB * HEAD_DIM
    return pl.pallas_call(
        _proj_prompt_kernel,
        grid=(b, t // tm),
        in_specs=[
            pl.BlockSpec((1, tm, d), lambda bi, ti: (bi, ti, 0)),
            const(g), const(wrow), const(wt), const(bf),
            pl.BlockSpec((tm, w), lambda bi, ti: (ti, 0)),
            pl.BlockSpec((tm, w), lambda bi, ti: (ti, 0)),
            const(tri),
        ],
        out_specs=[rspec] * 6 + [cspec(w), cspec(w), cspec(nsb), cspec(nsb)] + [cspec(w)] * 4
        + [cspec(16)] * 4,
        out_shape=[row(BF16)] * 5 + [row(F32)]
        + [col(w, F32), col(w, F32), col(nsb, F32), col(nsb, F32)] + [col(w, BF16)] * 4
        + [col(16, F32)] + [col(16, BF16)] * 3,
        scratch_shapes=[pltpu.VMEM((16, 1), F32)],
        compiler_params=_cparams("parallel", "arbitrary"),
        name="proj_prompt",
    )(x, g, wrow, wt, bf, cos, sin, tri)


def _proj_sample_kernel(x_ref, g_ref, w_ref, bf_ref, cos_ref, sin_ref,
                        qa_ref, qs_ref, qr_ref, kr_ref, vr_ref, gr_ref,
                        ka_ref, va_ref, ks_ref, vs_ref, lf_ref):
    w = GROUP_W
    xn = _rms(x_ref[...], g_ref[...]).astype(BF16)
    pr = _dot(xn, w_ref[...])
    cos = cos_ref[...]
    sin = sin_ref[...]
    qa_ref[...] = pr[:, 0:w] * Q_SCALE
    qs_ref[...] = pr[:, w:2 * w] * Q_SCALE
    qr_ref[...] = pr[:, 2 * w:3 * w] * cos + pr[:, 3 * w:4 * w] * sin
    kr_ref[...] = (pr[:, 4 * w:5 * w] * cos + pr[:, 5 * w:6 * w] * sin) * Q_SCALE
    vr_ref[...] = pr[:, 6 * w:7 * w]
    gr_ref[...] = pr[:, 7 * w:8 * w]
    ka_ref[...] = pr[:, 8 * w:9 * w]
    va_ref[...] = pr[:, 9 * w:10 * w]
    ks_ref[...] = pr[:, 10 * w:11 * w]
    vs_ref[...] = pr[:, 11 * w:12 * w]
    lf_ref[...] = jax.nn.log_sigmoid(pr[:, 12 * w:12 * w + 128] + bf_ref[...])


def _proj_sample(x, g, w_all, bf_row, cos, sin):
    m = x.shape[0]
    w = GROUP_W
    return pl.pallas_call(
        _proj_sample_kernel,
        out_shape=[jax.ShapeDtypeStruct((m, w), F32)] * 10 + [jax.ShapeDtypeStruct((m, 128), F32)],
        compiler_params=pltpu.CompilerParams(vmem_limit_bytes=VMEM_LIMIT_BYTES),
        name="proj_sample",
    )(x, g, w_all, bf_row, cos, sin)


def _merge_kernel(h_ref, of_ref, oret_ref, gr_ref, osb_ref, gn_ref, avg_ref,
                  w1_ref, w2_ref, w3_ref, g_ref, o_ref):
    x = oret_ref[...]
    avg = avg_ref[...]
    hi, mid, lo = _split3(x)
    mu = _dot(hi, avg) + _dot(mid, avg) + _dot(lo, avg)
    dev = x - mu
    var = _dot_acc3(dev * dev, avg)
    gate = gr_ref[...]
    r = dev * lax.rsqrt(var + EPS) * gn_ref[...] * (gate * jax.nn.sigmoid(gate))
    y = _dot(of_ref[...], w1_ref[...]) + _dot(r.astype(BF16), w2_ref[...]) + _dot(osb_ref[...], w3_ref[...])
    o_ref[...] = h_ref[...] + _rms(y, g_ref[...])


def _merge_out(h, o_fox, o_ret, gr, o_sb, gn, avg, w1, w2, w3, g):
    m, d = h.shape
    w = GROUP_W
    tm = min(512, m)
    rows = lambda n: pl.BlockSpec((tm, n), lambda i: (i, 0))
    const = lambda a: pl.BlockSpec(a.shape, lambda i: (0,) * a.ndim)
    return pl.pallas_call(
        _merge_kernel,
        grid=(m // tm,),
        in_specs=[rows(d), rows(w), rows(w), rows(w), rows(w), const(gn), const(avg),
                  const(w1), const(w2), const(w3), const(g)],
        out_specs=rows(d),
        out_shape=jax.ShapeDtypeStruct((m, d), F32),
        compiler_params=_cparams("parallel"),
        name="merge_out",
    )(h, o_fox, o_ret, gr, o_sb, gn, avg, w1, w2, w3, g)


def _head_lane_mask(shape, hh):
    lane = lax.broadcasted_iota(jnp.int32, shape, len(shape) - 1)
    return (lane >= HEAD_DIM * hh) & (lane < HEAD_DIM * (hh + 1))


def _head_sublane_mask(shape, hh):
    r = lax.broadcasted_iota(jnp.int32, shape, 0)
    return (r >= HEAD_DIM * hh) & (r < HEAD_DIM * (hh + 1))


def _fox_prompt_kernel(kaug_ref, qt_ref, qaug_ref, vt_ref, o_ref, qa_ref, *, tq, tk):
    i = pl.program_id(2)
    qt2 = qt_ref[0]
    kdim = kaug_ref.shape[-1]
    zpad = jnp.zeros((kdim - 128 - AUG_ROWS, tq), BF16)
    for hh in range(HEADS_PER_BLOCK):
        qa_ref[hh] = jnp.concatenate(
            [jnp.where(_head_sublane_mask(qt2.shape, hh), qt2, jnp.zeros_like(qt2)), qaug_ref[0, hh], zpad], axis=0)
    krow = lax.broadcasted_iota(jnp.int32, (tk, tq), 0)
    qcol = lax.broadcasted_iota(jnp.int32, (tk, tq), 1)
    sub = tq // tk

    def step(j, carry, diag):
        off = pl.multiple_of(j * tk, tk)
        kt = kaug_ref[0, 0, pl.ds(off, tk), :]
        new = []
        for hh in range(HEADS_PER_BLOCK):
            m, acc = carry[hh]
            s = _dot(kt, qa_ref[hh])
            if diag is not None:
                s = jnp.where(krow + diag * tk <= qcol, s, NEG_INF)
            m_new = jnp.maximum(m, jnp.max(s, axis=0, keepdims=True))
            alpha = jnp.exp2(m - m_new)
            p = jnp.exp2(s - m_new).astype(BF16)
            vt = vt_ref[0, hh, :, pl.ds(off, tk)]
            new.append((m_new, alpha * acc + _dot(vt, p)))
        return tuple(new)

    init = tuple((jnp.full((1, tq), NEG_INF, F32), jnp.zeros((V_ROWS, tq), F32)) for _ in range(HEADS_PER_BLOCK))
    carry = lax.fori_loop(0, i * sub, functools.partial(step, diag=None), init)
    for d in range(sub):
        carry = step(i * sub + d, carry, d)
    o_t = jnp.concatenate([acc[0:HEAD_DIM] / acc[HEAD_DIM:HEAD_DIM + 1] for _, acc in carry], axis=0)
    o_ref[0] = o_t.T.astype(o_ref.dtype)


def _fox_prompt(kaug, qt, qaug, vtaug):
    b, nb, t, kdim = kaug.shape
    tq = min(PROMPT_TQ, t)
    tk = min(FOX_TK, tq)
    return pl.pallas_call(
        functools.partial(_fox_prompt_kernel, tq=tq, tk=tk),
        grid=(b, nb, t // tq),
        in_specs=[
            pl.BlockSpec((1, 1, t, kdim), lambda bi, p, i: (bi, p, 0, 0)),
            pl.BlockSpec((1, 128, tq), lambda bi, p, i: (bi, p, i)),
            pl.BlockSpec((1, HEADS_PER_BLOCK, AUG_ROWS, tq), lambda bi, p, i: (bi, p, 0, i)),
            pl.BlockSpec((1, HEADS_PER_BLOCK, V_ROWS, t), lambda bi, p, i: (bi, p, 0, 0)),
        ],
        out_specs=pl.BlockSpec((1, tq, 128), lambda bi, p, i: (bi, i, p)),
        out_shape=jax.ShapeDtypeStruct((b, t, nb * 128), BF16),
        scratch_shapes=[pltpu.VMEM((HEADS_PER_BLOCK, kdim, tq), BF16)],
        compiler_params=_cparams("parallel", "parallel", "arbitrary"),
        name="fox_prompt",
    )(kaug, qt, qaug, vtaug)


def _neg_abs(x):
    bits = lax.bitcast_convert_type(x, jnp.uint32) | jnp.uint32(0x80000000)
    return lax.bitcast_convert_type(bits, F32)


def _sb_prompt_kernel(k_ref, qt_ref, vt_ref, ut_ref, o_ref, q_ref, *, tq, tk):
    i = pl.program_id(2)
    qt2 = qt_ref[0]
    for hh in range(HEADS_PER_BLOCK):
        q_ref[hh] = jnp.where(_head_sublane_mask(qt2.shape, hh), qt2, jnp.zeros_like(qt2))
    ut2 = ut_ref[...]
    cb = ut2.shape[0]
    nsub = tk // cb
    krow = lax.broadcasted_iota(jnp.int32, (tk, tq), 0)
    qcol = lax.broadcasted_iota(jnp.int32, (tk, tq), 1)
    sub = tq // tk

    def step(j, carry, diag):
        off = pl.multiple_of(j * tk, tk)
        kt = k_ref[0, pl.ds(off, tk), :]
        new = []
        for hh in range(HEADS_PER_BLOCK):
            rest, acc = carry[hh]
            z = _dot(kt, q_ref[hh])
            pr = jnp.maximum(z, 0.0) + jnp.log2(1.0 + jnp.exp2(_neg_abs(z)))
            if diag is not None:
                valid = krow + diag * tk < qcol
                pr = jnp.where(valid, pr, 0.0)
            take = z - pr
            blocks = [None] * nsub
            for blk in reversed(range(nsub)):
                prb = pr[blk * cb:(blk + 1) * cb]
                hi = prb.astype(BF16)
                lo = (prb - hi.astype(F32)).astype(BF16)
                later = _dot(ut2, jnp.concatenate([hi, lo], axis=0))
                blocks[blk] = jnp.exp2(take[blk * cb:(blk + 1) * cb] + later + rest)
                rest = rest + later[0:1] - prb[0:1]
            a = blocks[0] if nsub == 1 else jnp.concatenate(blocks, axis=0)
            if diag is not None:
                a = jnp.where(valid, a, 0.0)
            vt = vt_ref[0, HEAD_DIM * hh:HEAD_DIM * (hh + 1), pl.ds(off, tk)]
            new.append((rest, acc + _dot(vt, a.astype(BF16))))
        return tuple(new)

    carry = tuple((jnp.zeros((1, tq), F32), jnp.zeros((HEAD_DIM, tq), F32)) for _ in range(HEADS_PER_BLOCK))
    for d in reversed(range(sub)):
        carry = step(i * sub + d, carry, d)
    final = lax.fori_loop(0, i * sub, lambda n, c: step(i * sub - 1 - n, c, None), carry)
    o_t = jnp.concatenate([acc for _, acc in final], axis=0)
    o_ref[0] = o_t.T.astype(o_ref.dtype)


def _sb_prompt(krows, qt, vt):
    b, t, w = krows.shape
    tq = min(PROMPT_TQ, t)
    tk = min(SB_TK, tq)
    cb = min(CUM_BLOCK, tk)
    nb = w // 128
    ut = -(jnp.arange(cb)[None, :] > jnp.arange(cb)[:, None]).astype(BF16)
    ut = jnp.concatenate([ut, ut], axis=1)
    return pl.pallas_call(
        functools.partial(_sb_prompt_kernel, tq=tq, tk=tk),
        grid=(b, nb, t // tq),
        in_specs=[
            pl.BlockSpec((1, t, 128), lambda bi, p, i: (bi, 0, p)),
            pl.BlockSpec((1, 128, tq), lambda bi, p, i: (bi, p, i)),
            pl.BlockSpec((1, 128, t), lambda bi, p, i: (bi, p, 0)),
            pl.BlockSpec((cb, 2 * cb), lambda bi, p, i: (0, 0)),
        ],
        out_specs=pl.BlockSpec((1, tq, 128), lambda bi, p, i: (bi, i, p)),
        out_shape=jax.ShapeDtypeStruct((b, t, w), BF16),
        scratch_shapes=[pltpu.VMEM((HEADS_PER_BLOCK, 128, tq), BF16)],
        compiler_params=_cparams("parallel", "parallel", "arbitrary"),
        name="sb_prompt",
    )(krows, qt, vt, ut)


def _log_gamma(head):
    pw = jnp.zeros(head.shape, F32)
    for h in range(HEADS_PER_BLOCK * (GROUP_W // 128)):
        pw = jnp.where(head == h, 2.0 ** (-5.0 - h), pw)
    return jnp.log(1.0 - pw)


def _ret_prompt_kernel(q_ref, k_ref, v_ref, o_ref, s_ref, state_ref, dec_ref, qkd_ref, *, ch):
    p = pl.program_id(1)
    c = pl.program_id(2)

    @pl.when(c == 0)
    def _():
        state_ref[...] = jnp.zeros_like(state_ref)
        ii = lax.broadcasted_iota(jnp.int32, (ch, ch), 0)
        jj = lax.broadcasted_iota(jnp.int32, (ch, ch), 1)
        diff = ii - jj
        pos = lax.broadcasted_iota(jnp.int32, (ch, 1), 0).astype(F32)
        for hh in range(HEADS_PER_BLOCK):
            lg = _log_gamma(jnp.full((1, 1), HEADS_PER_BLOCK * p + hh, jnp.int32))
            dec_ref[hh] = jnp.where(diff >= 0, jnp.exp(jnp.maximum(diff, 0).astype(F32) * lg), 0.0)
            qkd_ref[hh, :, 0:1] = jnp.exp((pos + 1.0) * lg)
            qkd_ref[hh, :, 1:2] = jnp.exp((ch - 1.0 - pos) * lg)
            qkd_ref[hh, :, 2:3] = jnp.broadcast_to(jnp.exp(ch * lg), (ch, 1))

    q2 = q_ref[0]
    k2 = k_ref[0]
    v2 = v_ref[0]
    outs = []
    for hh in range(HEADS_PER_BLOCK):
        qh = jnp.where(_head_lane_mask(q2.shape, hh), q2, jnp.zeros_like(q2))
        scores = _dot_nt(qh, k2) * dec_ref[hh]
        inner = _dot(scores.astype(BF16), v2)
        state = state_ref[hh]
        q_dec = (qh.astype(F32) * qkd_ref[hh, :, 0:1]).astype(BF16)
        cross = _dot(q_dec, state.astype(BF16))
        outs.append(inner + cross)
        k_dec = (k2.astype(F32) * qkd_ref[hh, :, 1:2]).astype(BF16)
        state_ref[hh] = qkd_ref[hh, 0:1, 2:3] * state + _dot_tn(k_dec, v2)
    o_ref[0] = jnp.where(_head_lane_mask(outs[0].shape, 0), outs[0], outs[1])

    @pl.when(c == pl.num_programs(2) - 1)
    def _():
        s_ref[0, 0] = state_ref[...]


def _ret_prompt(q, k, v):
    b, t, w = q.shape
    ch = min(256, t)
    nb = w // 128
    tile = pl.BlockSpec((1, ch, 128), lambda bi, p, c: (bi, c, p))
    return pl.pallas_call(
        functools.partial(_ret_prompt_kernel, ch=ch),
        grid=(b, nb, t // ch),
        in_specs=[tile, tile, tile],
        out_specs=[tile, pl.BlockSpec((1, 1, HEADS_PER_BLOCK, 128, 128), lambda bi, p, c: (bi, p, 0, 0, 0))],
        out_shape=[jax.ShapeDtypeStruct((b, t, w), F32),
                   jax.ShapeDtypeStruct((b, nb, HEADS_PER_BLOCK, 128, 128), F32)],
        scratch_shapes=[pltpu.VMEM((HEADS_PER_BLOCK, 128, 128), F32),
                        pltpu.VMEM((HEADS_PER_BLOCK, ch, ch), F32),
                        pltpu.VMEM((HEADS_PER_BLOCK, ch, 3), F32)],
        compiler_params=_cparams("parallel", "parallel", "arbitrary"),
        name="ret_prompt",
    )(q, k, v)


def _ret_sample_kernel(q_ref, k_ref, v_ref, s_ref, o_ref, snew_ref, *, n_new):
    q = q_ref[0]
    k = k_ref[0]
    v = v_ref[0]
    ii = lax.broadcasted_iota(jnp.int32, (n_new, n_new), 0)
    jj = lax.broadcasted_iota(jnp.int32, (n_new, n_new), 1)
    diff = ii - jj
    pos = lax.broadcasted_iota(jnp.int32, (n_new, 1), 0).astype(F32)
    o_ref[0] = jnp.zeros(o_ref.shape[1:], F32)
    for h in range(H_RET):
        lg = _log_gamma(jnp.full((1, 1), h, jnp.int32))
        sl = slice(h * HEAD_DIM, (h + 1) * HEAD_DIM)
        qh = q[:, sl]
        kh = k[:, sl]
        vh = v[:, sl].astype(BF16)
        decay = jnp.where(diff >= 0, jnp.exp(jnp.maximum(diff, 0).astype(F32) * lg), 0.0)
        scores = _dot_nt(qh.astype(BF16), kh.astype(BF16)) * decay
        inner = _dot(scores.astype(BF16), vh)
        state = s_ref[0, h]
        cross = _dot((qh * jnp.exp((pos + 1.0) * lg)).astype(BF16), state.astype(BF16))
        o_ref[0, :, sl] = inner + cross
        k_dec = (kh * jnp.exp((n_new - 1.0 - pos) * lg)).astype(BF16)
        snew_ref[0, h] = jnp.exp(n_new * lg) * state + _dot_tn(k_dec, vh)


def _ret_sample(q, k, v, state):
    nb, n_new, w = q.shape
    tok = pl.BlockSpec((1, n_new, w), lambda i: (i, 0, 0))
    st = pl.BlockSpec((1, H_RET, HEAD_DIM, HEAD_DIM), lambda i: (i, 0, 0, 0))
    return pl.pallas_call(
        functools.partial(_ret_sample_kernel, n_new=n_new),
        grid=(nb,),
        in_specs=[tok, tok, tok, st],
        out_specs=[tok, st],
        out_shape=[jax.ShapeDtypeStruct((nb, n_new, w), F32),
                   jax.ShapeDtypeStruct(state.shape, F32)],
        compiler_params=_cparams("parallel"),
        name="ret_sample",
    )(q, k, v, state)


ROWS_PER_TOKEN = 8


def _row_token(shape):
    return lax.broadcasted_iota(jnp.int32, shape, 0) // ROWS_PER_TOKEN


def _collapse_heads(acc, n_new, width):
    r = lax.broadcasted_iota(jnp.int32, acc.shape, 0) % ROWS_PER_TOKEN
    lane_head = lax.broadcasted_iota(jnp.int32, acc.shape, 1) // HEAD_DIM
    kept = jnp.where(r == lane_head, acc, 0.0)
    return jnp.sum(kept.reshape(n_new, ROWS_PER_TOKEN, width), axis=1)


def _fox_decode_kernel(pt_ref, qbd_ref, knew_ref, vnew_ref, lfcol_ref, lfmat_ref, lfc_ref, u_ref, *rest,
                       n_pp, n_new, n_pages):
    k_refs = rest[:n_pp]
    v_refs = rest[n_pp:2 * n_pp]
    o_ref = rest[2 * n_pp]
    m_ref, l_ref, acc_ref, suf_ref, lf_ref = rest[2 * n_pp + 1:]
    b = pl.program_id(0)
    g = pl.program_id(1)
    nrow = n_new * ROWS_PER_TOKEN
    qbd = qbd_ref[0]
    u = u_ref[...]

    lfcol = lfcol_ref[0]
    cnew_col = jnp.concatenate(
        [sum(lfcol[j * ROWS_PER_TOKEN:(j + 1) * ROWS_PER_TOKEN] for j in range(t + 1)) for t in range(n_new)], axis=0)

    def attend(s, pv):
        m_new = jnp.maximum(m_ref[...], jnp.max(s, axis=1, keepdims=True))
        alpha = jnp.exp(m_ref[...] - m_new)
        p = jnp.exp(s - m_new)
        l_ref[...] = alpha * l_ref[...] + jnp.sum(p, axis=1, keepdims=True)
        acc_ref[...] = alpha * acc_ref[...] + pv(p.astype(BF16))
        m_ref[...] = m_new

    @pl.when(g == 0)
    def _():
        m_ref[...] = jnp.full_like(m_ref, NEG_INF)
        l_ref[...] = jnp.zeros_like(l_ref)
        acc_ref[...] = jnp.zeros_like(acc_ref)
        suf_ref[...] = jnp.zeros_like(suf_ref)
        lf_ref[...] = jnp.zeros_like(lf_ref)
        lane = lax.broadcasted_iota(jnp.int32, (nrow, 128), 1)
        lfmat = lfmat_ref[0]
        cnew_mat = jnp.zeros_like(lfmat)
        for j in range(n_new):
            cnew_mat = cnew_mat + jnp.where(lane >= j, lfmat[:, j:j + 1], 0.0)
        s = _dot(qbd, knew_ref[0]) + (cnew_col - cnew_mat)
        s = jnp.where(lane <= _row_token((nrow, 128)), s, NEG_INF)
        attend(s, lambda p: _dot(p, vnew_ref[0]))

    for ii in range(n_pp):
        page = pt_ref[b, n_pages - 1 - (g * n_pp + ii)]
        for h in range(H_FOX):
            lf_ref[pl.ds(ii * ROWS_PER_TOKEN + h, 1), :] = lfc_ref[0, h, pl.ds(page, 1), :]
    lf_all = lf_ref[...]
    within = _dot_acc3(lf_all, u)
    totals = jnp.sum(lf_all, axis=1, keepdims=True)
    later_pages = suf_ref[...]
    scores = []
    for ii in range(n_pp):
        sl = slice(ii * ROWS_PER_TOKEN, (ii + 1) * ROWS_PER_TOKEN)
        bias = jnp.concatenate([within[sl] + later_pages] * n_new, axis=0) + cnew_col
        kp = k_refs[ii][0, 0].reshape(H_FOX * HEAD_DIM, 128).astype(BF16)
        scores.append(_dot(qbd, kp) + bias)
        later_pages = later_pages + totals[sl]
    suf_ref[...] = later_pages

    def pv(p):
        out = None
        for ii in range(n_pp):
            vp = v_refs[ii][0, 0].reshape(H_FOX * HEAD_DIM, 128).astype(BF16)
            part = _dot_nt(p[:, ii * 128:(ii + 1) * 128], vp)
            out = part if out is None else out + part
        return out

    attend(jnp.concatenate(scores, axis=1), pv)

    @pl.when(g == pl.num_programs(1) - 1)
    def _():
        o_ref[0] = _collapse_heads(acc_ref[...] / l_ref[...], n_new, GROUP_W)


def _pages_per_step(n_pages):
    return min(8, n_pages)


def _fox_decode(layer, page_table, qbd, knew_t, vnew, lfcol, lfmat, cache_k, cache_v, cache_lf):
    nb, nrow, w = qbd.shape
    n_new = nrow // ROWS_PER_TOKEN
    n_pages = page_table.shape[1]
    n_pp = _pages_per_step(n_pages)
    u = (jnp.arange(128)[:, None] > jnp.arange(128)[None, :]).astype(BF16)
    per_seq = lambda a: pl.BlockSpec((1,) + a.shape[1:], lambda b, g, pt: (b,) + (0,) * (a.ndim - 1))

    def page_spec(ii):
        return pl.BlockSpec((1, 1) + cache_k.shape[2:],
                            lambda b, g, pt: (layer, pt[b, n_pages - 1 - (g * n_pp + ii)], 0, 0, 0))

    grid_spec = pltpu.PrefetchScalarGridSpec(
        num_scalar_prefetch=1,
        grid=(nb, n_pages // n_pp),
        in_specs=[per_seq(qbd), per_seq(knew_t), per_seq(vnew), per_seq(lfcol), per_seq(lfmat),
                  pl.BlockSpec((1,) + cache_lf.shape[1:], lambda b, g, pt: (layer, 0, 0, 0)),
                  pl.BlockSpec(u.shape, lambda b, g, pt: (0, 0))]
        + [page_spec(ii) for ii in range(n_pp)] * 2,
        out_specs=pl.BlockSpec((1, n_new, w), lambda b, g, pt: (b, 0, 0)),
        scratch_shapes=[pltpu.VMEM((nrow, 1), F32), pltpu.VMEM((nrow, 1), F32), pltpu.VMEM((nrow, w), F32),
                        pltpu.VMEM((ROWS_PER_TOKEN, 1), F32), pltpu.VMEM((n_pp * ROWS_PER_TOKEN, 128), F32)],
    )
    return pl.pallas_call(
        functools.partial(_fox_decode_kernel, n_pp=n_pp, n_new=n_new, n_pages=n_pages),
        grid_spec=grid_spec,
        out_shape=jax.ShapeDtypeStruct((nb, n_new, w), F32),
        compiler_params=_cparams("parallel", "arbitrary"),
        name="fox_decode",
    )(page_table, qbd, knew_t, vnew, lfcol, lfmat, cache_lf, u, *([cache_k] * n_pp), *([cache_v] * n_pp))


def _sb_decode_kernel(pt_ref, qbd_ref, knew_ref, vnew_ref, u_ref, *rest, n_pp, n_new):
    k_refs = rest[:n_pp]
    v_refs = rest[n_pp:2 * n_pp]
    o_ref = rest[2 * n_pp]
    rest_ref, acc_ref = rest[2 * n_pp + 1:]
    g = pl.program_id(1)
    nrow = n_new * ROWS_PER_TOKEN
    width = H_SB * HEAD_DIM
    qbd = qbd_ref[0]
    u2 = u_ref[...]

    def weights(z_blocks, valid):
        n = len(z_blocks)
        z = jnp.concatenate(z_blocks, axis=0)
        pr = jnp.maximum(z, 0.0) + jnp.log2(1.0 + jnp.exp2(_neg_abs(z)))
        if valid is not None:
            pr = jnp.where(valid, pr, 0.0)
        hi = pr.astype(BF16)
        lo = (pr - hi.astype(F32)).astype(BF16)
        later = _dot(jnp.concatenate([hi, lo], axis=1), u2)
        run = rest_ref[...]
        out = []
        for ii in range(n):
            sl = slice(ii * nrow, (ii + 1) * nrow)
            a = jnp.exp2(z[sl] - pr[sl] + later[sl] + run)
            if valid is not None:
                a = jnp.where(valid, a, 0.0)
            out.append(a.astype(BF16))
            run = run + later[sl, 0:1] - pr[sl, 0:1]
        rest_ref[...] = run
        return out

    @pl.when(g == 0)
    def _():
        rest_ref[...] = jnp.zeros_like(rest_ref)
        acc_ref[...] = jnp.zeros_like(acc_ref)
        lane = lax.broadcasted_iota(jnp.int32, (nrow, 128), 1)
        (a,) = weights([_dot(qbd, knew_ref[0])], lane < _row_token((nrow, 128)))
        acc_ref[...] = _dot(a, vnew_ref[0])

    z_blocks = [_dot(qbd, k_refs[ii][0, 0].reshape(width, 128).astype(BF16)) for ii in range(n_pp)]
    a_blocks = weights(z_blocks, None)
    out = acc_ref[...]
    for ii in range(n_pp):
        out = out + _dot_nt(a_blocks[ii], v_refs[ii][0, 0].reshape(width, 128).astype(BF16))
    acc_ref[...] = out

    @pl.when(g == pl.num_programs(1) - 1)
    def _():
        o_ref[0] = _collapse_heads(acc_ref[...], n_new, width)


def _sb_decode(layer, page_table, qbd, knew_t, vnew, cache_k, cache_v):
    nb, nrow, width = qbd.shape
    n_new = nrow // ROWS_PER_TOKEN
    n_pages = page_table.shape[1]
    n_pp = _pages_per_step(n_pages)
    u = -(jnp.arange(128)[:, None] > jnp.arange(128)[None, :]).astype(BF16)
    u = jnp.concatenate([u, u], axis=0)
    per_seq = lambda a: pl.BlockSpec((1,) + a.shape[1:], lambda b, g, pt: (b,) + (0,) * (a.ndim - 1))

    def page_spec(ii):
        return pl.BlockSpec((1, 1) + cache_k.shape[2:],
                            lambda b, g, pt: (layer, pt[b, n_pages - 1 - (g * n_pp + ii)], 0, 0, 0))

    grid_spec = pltpu.PrefetchScalarGridSpec(
        num_scalar_prefetch=1,
        grid=(nb, n_pages // n_pp),
        in_specs=[per_seq(qbd), per_seq(knew_t), per_seq(vnew), pl.BlockSpec(u.shape, lambda b, g, pt: (0, 0))]
        + [page_spec(ii) for ii in range(n_pp)] * 2,
        out_specs=pl.BlockSpec((1, n_new, width), lambda b, g, pt: (b, 0, 0)),
        scratch_shapes=[pltpu.VMEM((nrow, 1), F32), pltpu.VMEM((nrow, width), F32)],
    )
    return pl.pallas_call(
        functools.partial(_sb_decode_kernel, n_pp=n_pp, n_new=n_new),
        grid_spec=grid_spec,
        out_shape=jax.ShapeDtypeStruct((nb, n_new, width), F32),
        compiler_params=_cparams("parallel", "arbitrary"),
        name="sb_decode",
    )(page_table, qbd, knew_t, vnew, u, *([cache_k] * n_pp), *([cache_v] * n_pp))


def _pad_cols(a, width=GROUP_W):
    return jnp.pad(a, ((0, 0), (0, width - a.shape[1])))


def _split_mix_weight(w):
    wf, wr, ws = H_FOX * HEAD_DIM, H_RET * HEAD_DIM, H_SB * HEAD_DIM
    sizes = (wf, wf, wf, H_FOX, wr, wr, wr, wr, ws, ws, ws)
    offs = np.cumsum((0,) + sizes)
    return [w[:, offs[i]:offs[i + 1]] for i in range(len(sizes))]


def _rope_swap_perm(n_heads):
    idx = np.arange(n_heads * HEAD_DIM)
    return (idx // HEAD_DIM) * HEAD_DIM + (idx % HEAD_DIM + HEAD_DIM // 2) % HEAD_DIM


def _rope_tables(pos, n_heads):
    half = HEAD_DIM // 2
    inv_freq = ROPE_BASE ** (-jnp.arange(half, dtype=F32) / half)
    ang = pos.astype(F32)[:, None] * inv_freq[None, :]
    cos, sin = jnp.cos(ang), jnp.sin(ang)
    cos_t = jnp.tile(jnp.concatenate([cos, cos], axis=1), (1, n_heads))
    sin_t = jnp.tile(jnp.concatenate([-sin, sin], axis=1), (1, n_heads))
    return _pad_cols(cos_t), _pad_cols(sin_t)


def _block_diag_queries(q, n_heads):
    nb, n_new, w = q.shape
    slot = jnp.arange(ROWS_PER_TOKEN)[:, None]
    lane_head = (jnp.arange(w) // HEAD_DIM)[None, :]
    mask = (slot == lane_head) & (slot < n_heads)
    qb = jnp.where(mask[None, None], q[:, :, None, :], 0.0)
    return qb.reshape(nb, n_new * ROWS_PER_TOKEN, w).astype(BF16)


def _new_keys_t(k, width):
    nb, n_new, _ = k.shape
    kt = jnp.swapaxes(k[:, :, :width], 1, 2)
    return jnp.pad(kt, ((0, 0), (0, 0), (0, 128 - n_new))).astype(BF16)


def _new_values(v, width):
    nb, n_new, _ = v.shape
    return jnp.pad(v[:, :, :width], ((0, 0), (0, 128 - n_new), (0, 0))).astype(BF16)


def _fox_operands(kar, qat, vatb, chi, cmid, clo):
    b, t, w = kar.shape
    nb = w // 128
    csp = jnp.stack([chi, cmid, clo], axis=2)[:, :H_FOX]
    neg_c = jnp.transpose(-csp, (0, 3, 1, 2)).reshape(b, t, nb, HEADS_PER_BLOCK * 3)
    ones = jnp.ones((b, t, nb, 3), BF16)
    pad = jnp.zeros((b, t, nb, 128 - HEADS_PER_BLOCK * 3 - 3), BF16)
    kaug = jnp.concatenate([kar.reshape(b, t, nb, 128), neg_c, ones, pad], axis=-1)
    kaug = jnp.transpose(kaug, (0, 2, 1, 3))
    sel = np.kron(np.eye(HEADS_PER_BLOCK), np.ones((3, 1)))
    sel = jnp.asarray(np.tile(sel.T, (nb, 1)), BF16)
    sel = jnp.broadcast_to(sel[None, :, :, None], (b, H_FOX, HEADS_PER_BLOCK * 3, t))
    qaug = jnp.concatenate([sel, csp, jnp.zeros((b, H_FOX, AUG_ROWS - HEADS_PER_BLOCK * 3 - 3, t), BF16)], axis=2)
    vt = vatb.reshape(b, H_FOX, HEAD_DIM, t)
    vtaug = jnp.concatenate([vt, jnp.ones((b, H_FOX, 1, t), BF16),
                             jnp.zeros((b, H_FOX, V_ROWS - HEAD_DIM - 1, t), BF16)], axis=2)
    return kaug, qat, qaug, vtaug


def kernel(x_prompt, x_sample, cache_fox_k, cache_fox_v, cache_fox_logf, cache_sb_k, cache_sb_v, state_ret,
           page_table, norm_g, w_ffn1_in, w_ffn1_out, w_mix_in, b_forget, ret_gn_g, w_mix_out, w_ffn2_in,
           w_ffn2_out):
    depth = norm_g.shape[0]
    bsz, seq, d = x_prompt.shape
    nb, n_new, _ = x_sample.shape
    n_pages, page_size = page_table.shape[1], cache_fox_k.shape[2]
    past = n_pages * page_size
    wf, wr, ws = H_FOX * HEAD_DIM, H_RET * HEAD_DIM, H_SB * HEAD_DIM
    w = GROUP_W

    ck_fox = jnp.transpose(cache_fox_k, (0, 1, 3, 4, 2))
    cv_fox = jnp.transpose(cache_fox_v, (0, 1, 3, 4, 2))
    ck_sb = jnp.transpose(cache_sb_k, (0, 1, 3, 4, 2))
    cv_sb = jnp.transpose(cache_sb_v, (0, 1, 3, 4, 2))
    clf = jnp.transpose(cache_fox_logf, (0, 3, 1, 2))

    cos_p, sin_p = _rope_tables(jnp.arange(seq, dtype=jnp.int32), H_RET)
    cos_s, sin_s = _rope_tables(past + jnp.arange(n_new, dtype=jnp.int32), H_RET)
    cos_s, sin_s = jnp.tile(cos_s, (nb, 1)), jnp.tile(sin_s, (nb, 1))
    perm = _rope_swap_perm(H_RET)
    avg = jnp.asarray(np.kron(np.eye(w // HEAD_DIM), np.full((HEAD_DIM, HEAD_DIM), 1.0 / HEAD_DIM)), BF16)

    yp = x_prompt.reshape(bsz * seq, d)
    ys = x_sample.reshape(nb * n_new, d)
    st_p, st_s = [], []
    for l in range(depth):
        g = [norm_g[l, i][None, :] for i in range(norm_g.shape[1])]
        w1_in, w1_out = w_ffn1_in[l].astype(BF16), w_ffn1_out[l].astype(BF16)
        w2_in, w2_out = w_ffn2_in[l].astype(BF16), w_ffn2_out[l].astype(BF16)
        qa, ka, va, fa, qr, kr, vr, gr, qs, ks, vs = _split_mix_weight(w_mix_in[l])
        ret_cols = [_pad_cols(qr), _pad_cols(qr[:, perm]), _pad_cols(kr), _pad_cols(kr[:, perm]),
                    _pad_cols(vr), _pad_cols(gr)]
        kv_cols = [ka, va, _pad_cols(ks), _pad_cols(vs)]
        wrow = jnp.concatenate([ka, _pad_cols(ks)] + ret_cols, axis=1).astype(BF16)
        wt = jnp.concatenate(kv_cols + [qa, _pad_cols(qs), _pad_cols(fa, 16)], axis=1).T.astype(BF16)
        w_samp = jnp.concatenate([qa, _pad_cols(qs)] + ret_cols + kv_cols + [_pad_cols(fa, 128)], axis=1).astype(BF16)
        bf_col = jnp.pad(b_forget[l], (0, 16 - H_FOX))[:, None]
        bf_row = jnp.pad(b_forget[l], (0, 128 - H_FOX))[None, :]
        gn = _pad_cols(ret_gn_g[l][None, :])
        wo = w_mix_out[l]
        wo1 = wo[:wf].astype(BF16)
        wo2 = jnp.pad(wo[wf:wf + wr], ((0, w - wr), (0, 0))).astype(BF16)
        wo3 = jnp.pad(wo[wf + wr:], ((0, w - ws), (0, 0))).astype(BF16)

        hp = _ffn_block(yp, g[0], w1_in, w1_out, g[1])
        (kar, ksr, qr_p, kr_p, vr_p, gr_p, kat, vat, kst, vst, vatb, vstb, qat, qst, lft, chi, cmid, clo) = _proj_prompt(
            hp.reshape(bsz, seq, d), g[2], wrow, wt, bf_col, cos_p, sin_p)
        o_fox = _fox_prompt(*_fox_operands(kar, qat, vatb, chi, cmid, clo))
        o_ret, s_raw = _ret_prompt(qr_p, kr_p, vr_p)
        o_sb = _sb_prompt(ksr, qst, vstb)
        hp = _merge_out(hp, o_fox.reshape(bsz * seq, w), o_ret.reshape(bsz * seq, w), gr_p.reshape(bsz * seq, w),
                        o_sb.reshape(bsz * seq, w), gn, avg, wo1, wo2, wo3, g[3])
        yp = _ffn_block(hp, g[4], w2_in, w2_out, g[5])
        s_fin = jnp.stack([s_raw[:, h // 2, h % 2, (h % 2) * HEAD_DIM:(h % 2 + 1) * HEAD_DIM,
                                 (h % 2) * HEAD_DIM:(h % 2 + 1) * HEAD_DIM] for h in range(H_RET)], axis=1)
        to_heads = lambda a, nh: jnp.transpose(a.reshape(bsz, nh, HEAD_DIM, seq), (0, 3, 1, 2))
        st_p.append((to_heads(kat, H_FOX), to_heads(vat, H_FOX), jnp.transpose(lft[:, :H_FOX], (0, 2, 1)),
                     to_heads(kst, H_SB), to_heads(vst, H_SB), s_fin))

        hs = _ffn_block(ys, g[0], w1_in, w1_out, g[1])
        (qa_s, qs_s, qr_s, kr_s, vr_s, gr_s, ka_s, va_s, ks_s, vs_s, lf_s) = _proj_sample(
            hs, g[2], w_samp, bf_row, cos_s, sin_s)
        seqs = lambda a: a.reshape(nb, n_new, a.shape[-1])
        lf_new = seqs(lf_s)[:, :, :ROWS_PER_TOKEN]
        lfcol = lf_new.reshape(nb, n_new * ROWS_PER_TOKEN, 1)
        lfmat = jnp.broadcast_to(jnp.transpose(lf_new, (0, 2, 1))[:, None], (nb, n_new, ROWS_PER_TOKEN, n_new))
        lfmat = jnp.pad(lfmat.reshape(nb, n_new * ROWS_PER_TOKEN, n_new), ((0, 0), (0, 0), (0, 128 - n_new)))
        o_fox_s = _fox_decode(l, page_table, _block_diag_queries(seqs(qa_s), H_FOX), _new_keys_t(seqs(ka_s), wf),
                              _new_values(seqs(va_s), wf), lfcol, lfmat, ck_fox, cv_fox, clf)
        o_sb_s = _sb_decode(l, page_table, _block_diag_queries(seqs(qs_s)[:, :, :ws] * LOG2E, H_SB),
                            _new_keys_t(seqs(ks_s), ws), _new_values(seqs(vs_s), ws), ck_sb, cv_sb)
        o_ret_s, s_new = _ret_sample(seqs(qr_s), seqs(kr_s), seqs(vr_s), state_ret[l])
        hs = _merge_out(hs, o_fox_s.reshape(nb * n_new, w).astype(BF16), o_ret_s.reshape(nb * n_new, w), gr_s,
                        _pad_cols(o_sb_s.reshape(nb * n_new, ws)).astype(BF16), gn, avg, wo1, wo2, wo3, g[3])
        ys = _ffn_block(hs, g[4], w2_in, w2_out, g[5])
        heads = lambda a, nh: a[:, :nh * HEAD_DIM].reshape(nb, n_new, nh, HEAD_DIM)
        st_s.append((heads(ka_s, H_FOX), heads(va_s, H_FOX), seqs(lf_s)[:, :, :H_FOX],
                     heads(ks_s, H_SB), heads(vs_s, H_SB), s_new))

    stk = lambda sts, i: jnp.stack([s[i] for s in sts], axis=0)
    return (yp.reshape(bsz, seq, d), ys.reshape(nb, n_new, d),
            stk(st_p, 0), stk(st_p, 1), stk(st_p, 2), stk(st_p, 3), stk(st_p, 4), stk(st_p, 5),
            stk(st_s, 0), stk(st_s, 1), stk(st_s, 2), stk(st_s, 3), stk(st_s, 4), stk(st_s, 5))
```

```python
import functools
import math

import numpy as np
import jax
import jax.numpy as jnp
from jax import lax
from jax.experimental import pallas as pl
from jax.experimental.pallas import tpu as pltpu

HEAD_DIM = 64
H_FOX = 6
H_RET = 5
H_SB = 5
GROUP_W = 384
HEADS_PER_BLOCK = 2
EPS = 1e-6
ROPE_BASE = 10000.0
Q_SCALE = HEAD_DIM ** -0.5
LOG2E = math.log2(math.e)
AUG_ROWS = 16
V_ROWS = 80
CUM_BLOCK = 256
PROMPT_TQ = 1024
FOX_TK = 1024
SB_TK = 1024
VMEM_LIMIT_BYTES = 48 * 1024 * 1024
F32 = jnp.float32
BF16 = jnp.bfloat16
NEG_INF = float("-inf")


def _cparams(*sem):
    return pltpu.CompilerParams(dimension_semantics=sem, vmem_limit_bytes=VMEM_LIMIT_BYTES)


def _dot(a, b):
    return jnp.dot(a, b, preferred_element_type=F32)


def _dot_nt(a, b):
    return lax.dot_general(a, b, (((1,), (1,)), ((), ())), preferred_element_type=F32)


def _dot_tn(a, b):
    return lax.dot_general(a, b, (((0,), (0,)), ((), ())), preferred_element_type=F32)


def _split3(x):
    hi = x.astype(BF16)
    r = x - hi.astype(F32)
    mid = r.astype(BF16)
    lo = (r - mid.astype(F32)).astype(BF16)
    return hi, mid, lo


def _dot_acc3(x, m):
    hi, mid, lo = _split3(x)
    return _dot(hi, m) + _dot(mid, m) + _dot(lo, m)


def _rms(x, g):
    return x * lax.rsqrt(jnp.mean(x * x, axis=-1, keepdims=True) + EPS) * g


def _softplus_parts(z):
    lp = jnp.log1p(jnp.exp(-jnp.abs(z)))
    return -jnp.maximum(z, 0.0) - lp, jnp.minimum(z, 0.0) - lp


def _ffn_kernel(x_ref, gpre_ref, wg_ref, wu_ref, wo_ref, gpost_ref, o_ref, xn_ref, acc_ref):
    j = pl.program_id(1)

    @pl.when(j == 0)
    def _():
        xn_ref[...] = _rms(x_ref[...], gpre_ref[...]).astype(BF16)
        acc_ref[...] = jnp.zeros_like(acc_ref)

    xn = xn_ref[...]
    gate = _dot(xn, wg_ref[...])
    up = _dot(xn, wu_ref[...])
    hidden = (gate * jax.nn.sigmoid(gate) * up).astype(BF16)
    acc_ref[...] += _dot(hidden, wo_ref[...])

    @pl.when(j == pl.num_programs(1) - 1)
    def _():
        o_ref[...] = x_ref[...] + 0.5 * _rms(acc_ref[...], gpost_ref[...])


def _ffn_block(x, g_pre, w_in, w_out, g_post):
    m, d = x.shape
    dff = w_out.shape[0]
    tm = min(512, m)
    nf = 2
    tf = dff // nf
    return pl.pallas_call(
        _ffn_kernel,
        grid=(m // tm, nf),
        in_specs=[
            pl.BlockSpec((tm, d), lambda i, j: (i, 0)),
            pl.BlockSpec((1, d), lambda i, j: (0, 0)),
            pl.BlockSpec((d, tf), lambda i, j: (0, j)),
            pl.BlockSpec((d, tf), lambda i, j: (0, nf + j)),
            pl.BlockSpec((tf, d), lambda i, j: (j, 0)),
            pl.BlockSpec((1, d), lambda i, j: (0, 0)),
        ],
        out_specs=pl.BlockSpec((tm, d), lambda i, j: (i, 0)),
        out_shape=jax.ShapeDtypeStruct((m, d), F32),
        scratch_shapes=[pltpu.VMEM((tm, d), BF16), pltpu.VMEM((tm, d), F32)],
        compiler_params=_cparams("parallel", "arbitrary"),
        name="ffn_block",
    )(x, g_pre, w_in, w_in, w_out, g_post)


def _proj_prompt_kernel(x_ref, g_ref, wrow_ref, wt_ref, bf_ref, cos_ref, sin_ref, tri_ref, kplace_ref, qplace_ref,
                        ksr_ref, qr_ref, kr_ref, vr_ref, gr_ref, kaug_ref,
                        kat_ref, vat_ref, kst_ref, vst_ref,
                        vstb_ref, qat_ref, qst_ref,
                        lf_ref, qaug_ref, vtaug_ref, carry_ref):
    w = GROUP_W
    t = pl.program_id(1)

    @pl.when(t == 0)
    def _():
        carry_ref[...] = jnp.zeros_like(carry_ref)

    xn = _rms(x_ref[0], g_ref[...]).astype(BF16)
    pr = _dot(xn, wrow_ref[...])
    cos = cos_ref[...]
    sin = sin_ref[...]
    ksr_ref[0] = pr[:, w:2 * w].astype(BF16)
    qr_ref[0] = (pr[:, 2 * w:3 * w] * cos + pr[:, 3 * w:4 * w] * sin).astype(BF16)
    kr_ref[0] = ((pr[:, 4 * w:5 * w] * cos + pr[:, 5 * w:6 * w] * sin) * Q_SCALE).astype(BF16)
    vr_ref[0] = pr[:, 6 * w:7 * w].astype(BF16)
    gr_ref[0] = pr[:, 7 * w:8 * w]

    pt = _dot_nt(wt_ref[...], xn)
    kat = pt[0:w]
    vat = pt[w:2 * w]
    kst = pt[2 * w:3 * w]
    vst = pt[3 * w:4 * w]
    kat_ref[0] = kat
    vat_ref[0] = vat
    kst_ref[0] = kst[0:H_SB * HEAD_DIM]
    vst_ref[0] = vst[0:H_SB * HEAD_DIM]
    vstb_ref[0] = vst.astype(BF16)
    qat_ref[0] = (pt[4 * w:5 * w] * (Q_SCALE * LOG2E)).astype(BF16)
    qst_ref[0] = (pt[5 * w:6 * w] * (Q_SCALE * LOG2E)).astype(BF16)

    logf = jax.nn.log_sigmoid(pt[6 * w:6 * w + 16] + bf_ref[...])
    lf_ref[0] = logf
    c = _dot_acc3(logf, tri_ref[...]) + carry_ref[...]
    carry_ref[...] = c[:, -1:]

    tm = c.shape[1]
    one_row = jnp.concatenate([jnp.ones((1, tm), BF16), jnp.zeros((15, tm), BF16)], axis=0)
    csplit = jnp.concatenate(list(_split3(c * LOG2E)) + [one_row], axis=0)
    for p in range(w // 128):
        kaug_ref[0, p, :, 0:128] = pr[:, p * 128:(p + 1) * 128].astype(BF16)
        kaug_ref[0, p, :, 128:256] = _dot_tn(csplit, kplace_ref[p]).astype(BF16)
    q_rows = _dot(qplace_ref[...], csplit).astype(BF16)
    for h in range(H_FOX):
        qaug_ref[0, h] = q_rows[h * AUG_ROWS:(h + 1) * AUG_ROWS]
        vtaug_ref[0, h, 0:HEAD_DIM, :] = vat[h * HEAD_DIM:(h + 1) * HEAD_DIM].astype(BF16)
        vtaug_ref[0, h, HEAD_DIM:V_ROWS, :] = one_row


def _fox_placements():
    nb = GROUP_W // 128
    kplace = np.zeros((nb, 64, 128), np.float32)
    qplace = np.zeros((H_FOX * AUG_ROWS, 64), np.float32)
    for h in range(H_FOX):
        p, hh = divmod(h, HEADS_PER_BLOCK)
        for sp in range(3):
            kplace[p, sp * 16 + h, hh * 3 + sp] = -1.0
            kplace[p, 48, HEADS_PER_BLOCK * 3 + sp] = 1.0
            qplace[h * AUG_ROWS + hh * 3 + sp, 48] = 1.0
            qplace[h * AUG_ROWS + HEADS_PER_BLOCK * 3 + sp, sp * 16 + h] = 1.0
    return jnp.asarray(kplace, BF16), jnp.asarray(qplace, BF16)


def _proj_prompt(x, g, wrow, wt, bf, cos, sin):
    b, t, d = x.shape
    tm = min(512, t)
    w = GROUP_W
    nb = w // 128
    tri = (jnp.arange(tm)[:, None] <= jnp.arange(tm)[None, :]).astype(BF16)
    kplace, qplace = _fox_placements()
    row = lambda dt: jax.ShapeDtypeStruct((b, t, w), dt)
    col = lambda n, dt: jax.ShapeDtypeStruct((b, n, t), dt)
    rspec = pl.BlockSpec((1, tm, w), lambda bi, ti: (bi, ti, 0))
    cspec = lambda n: pl.BlockSpec((1, n, tm), lambda bi, ti: (bi, 0, ti))
    const = lambda a: pl.BlockSpec(a.shape, lambda bi, ti: (0,) * a.ndim)
    nsb = H_SB * HEAD_DIM
    return pl.pallas_call(
        _proj_prompt_kernel,
        grid=(b, t // tm),
        in_specs=[
            pl.BlockSpec((1, tm, d), lambda bi, ti: (bi, ti, 0)),
            const(g), const(wrow), const(wt), const(bf),
            pl.BlockSpec((tm, w), lambda bi, ti: (ti, 0)),
            pl.BlockSpec((tm, w), lambda bi, ti: (ti, 0)),
            const(tri), const(kplace), const(qplace),
        ],
        out_specs=[rspec] * 5 + [pl.BlockSpec((1, nb, tm, 256), lambda bi, ti: (bi, 0, ti, 0))]
        + [cspec(w), cspec(w), cspec(nsb), cspec(nsb)] + [cspec(w)] * 3 + [cspec(16)]
        + [pl.BlockSpec((1, H_FOX, AUG_ROWS, tm), lambda bi, ti: (bi, 0, 0, ti)),
           pl.BlockSpec((1, H_FOX, V_ROWS, tm), lambda bi, ti: (bi, 0, 0, ti))],
        out_shape=[row(BF16)] * 4 + [row(F32), jax.ShapeDtypeStruct((b, nb, t, 256), BF16)]
        + [col(w, F32), col(w, F32), col(nsb, F32), col(nsb, F32)] + [col(w, BF16)] * 3 + [col(16, F32)]
        + [jax.ShapeDtypeStruct((b, H_FOX, AUG_ROWS, t), BF16), jax.ShapeDtypeStruct((b, H_FOX, V_ROWS, t), BF16)],
        scratch_shapes=[pltpu.VMEM((16, 1), F32)],
        compiler_params=_cparams("parallel", "arbitrary"),
        name="proj_prompt",
    )(x, g, wrow, wt, bf, cos, sin, tri, kplace, qplace)


def _proj_sample_kernel(x_ref, g_ref, w_ref, bf_ref, cos_ref, sin_ref,
                        qa_ref, qs_ref, qr_ref, kr_ref, vr_ref, gr_ref,
                        ka_ref, va_ref, ks_ref, vs_ref, lf_ref):
    w = GROUP_W
    xn = _rms(x_ref[...], g_ref[...]).astype(BF16)
    pr = _dot(xn, w_ref[...])
    cos = cos_ref[...]
    sin = sin_ref[...]
    qa_ref[...] = pr[:, 0:w] * Q_SCALE
    qs_ref[...] = pr[:, w:2 * w] * Q_SCALE
    qr_ref[...] = pr[:, 2 * w:3 * w] * cos + pr[:, 3 * w:4 * w] * sin
    kr_ref[...] = (pr[:, 4 * w:5 * w] * cos + pr[:, 5 * w:6 * w] * sin) * Q_SCALE
    vr_ref[...] = pr[:, 6 * w:7 * w]
    gr_ref[...] = pr[:, 7 * w:8 * w]
    ka_ref[...] = pr[:, 8 * w:9 * w]
    va_ref[...] = pr[:, 9 * w:10 * w]
    ks_ref[...] = pr[:, 10 * w:11 * w]
    vs_ref[...] = pr[:, 11 * w:12 * w]
    lf_ref[...] = jax.nn.log_sigmoid(pr[:, 12 * w:12 * w + 128] + bf_ref[...])


def _proj_sample(x, g, w_all, bf_row, cos, sin):
    m = x.shape[0]
    w = GROUP_W
    return pl.pallas_call(
        _proj_sample_kernel,
        out_shape=[jax.ShapeDtypeStruct((m, w), F32)] * 10 + [jax.ShapeDtypeStruct((m, 128), F32)],
        compiler_params=pltpu.CompilerParams(vmem_limit_bytes=VMEM_LIMIT_BYTES),
        name="proj_sample",
    )(x, g, w_all, bf_row, cos, sin)


def _merge_kernel(h_ref, of_ref, oret_ref, gr_ref, osb_ref, gn_ref, avg_ref,
                  w1_ref, w2_ref, w3_ref, g_ref, o_ref):
    x = oret_ref[...]
    avg = avg_ref[...]
    hi, mid, lo = _split3(x)
    mu = _dot(hi, avg) + _dot(mid, avg) + _dot(lo, avg)
    dev = x - mu
    var = _dot_acc3(dev * dev, avg)
    gate = gr_ref[...]
    r = dev * lax.rsqrt(var + EPS) * gn_ref[...] * (gate * jax.nn.sigmoid(gate))
    y = _dot(of_ref[...], w1_ref[...]) + _dot(r.astype(BF16), w2_ref[...]) + _dot(osb_ref[...], w3_ref[...])
    o_ref[...] = h_ref[...] + _rms(y, g_ref[...])


def _merge_out(h, o_fox, o_ret, gr, o_sb, gn, avg, w1, w2, w3, g):
    m, d = h.shape
    w = GROUP_W
    tm = min(512, m)
    rows = lambda n: pl.BlockSpec((tm, n), lambda i: (i, 0))
    const = lambda a: pl.BlockSpec(a.shape, lambda i: (0,) * a.ndim)
    return pl.pallas_call(
        _merge_kernel,
        grid=(m // tm,),
        in_specs=[rows(d), rows(w), rows(w), rows(w), rows(w), const(gn), const(avg),
                  const(w1), const(w2), const(w3), const(g)],
        out_specs=rows(d),
        out_shape=jax.ShapeDtypeStruct((m, d), F32),
        compiler_params=_cparams("parallel"),
        name="merge_out",
    )(h, o_fox, o_ret, gr, o_sb, gn, avg, w1, w2, w3, g)


def _head_lane_mask(shape, hh):
    lane = lax.broadcasted_iota(jnp.int32, shape, len(shape) - 1)
    return (lane >= HEAD_DIM * hh) & (lane < HEAD_DIM * (hh + 1))


def _head_sublane_mask(shape, hh):
    r = lax.broadcasted_iota(jnp.int32, shape, 0)
    return (r >= HEAD_DIM * hh) & (r < HEAD_DIM * (hh + 1))


def _fox_prompt_kernel(kaug_ref, qt_ref, qaug_ref, vt_ref, o_ref, qa_ref, *, tq, tk):
    i = pl.program_id(2)
    qt2 = qt_ref[0]
    kdim = kaug_ref.shape[-1]
    zpad = jnp.zeros((kdim - 128 - AUG_ROWS, tq), BF16)
    for hh in range(HEADS_PER_BLOCK):
        qa_ref[hh] = jnp.concatenate(
            [jnp.where(_head_sublane_mask(qt2.shape, hh), qt2, jnp.zeros_like(qt2)), qaug_ref[0, hh], zpad], axis=0)
    krow = lax.broadcasted_iota(jnp.int32, (tk, tq), 0)
    qcol = lax.broadcasted_iota(jnp.int32, (tk, tq), 1)
    sub = tq // tk

    def step(j, carry, diag):
        off = pl.multiple_of(j * tk, tk)
        kt = kaug_ref[0, 0, pl.ds(off, tk), :]
        new = []
        for hh in range(HEADS_PER_BLOCK):
            m, acc = carry[hh]
            s = _dot(kt, qa_ref[hh])
            if diag is not None:
                s = jnp.where(krow + diag * tk <= qcol, s, NEG_INF)
            m_new = jnp.maximum(m, jnp.max(s, axis=0, keepdims=True))
            alpha = jnp.exp2(m - m_new)
            p = jnp.exp2(s - m_new).astype(BF16)
            vt = vt_ref[0, hh, :, pl.ds(off, tk)]
            new.append((m_new, alpha * acc + _dot(vt, p)))
        return tuple(new)

    init = tuple((jnp.full((1, tq), NEG_INF, F32), jnp.zeros((V_ROWS, tq), F32)) for _ in range(HEADS_PER_BLOCK))
    carry = lax.fori_loop(0, i * sub, functools.partial(step, diag=None), init)
    for d in range(sub):
        carry = step(i * sub + d, carry, d)
    o_t = jnp.concatenate([acc[0:HEAD_DIM] / acc[HEAD_DIM:HEAD_DIM + 1] for _, acc in carry], axis=0)
    o_ref[0] = o_t.T.astype(o_ref.dtype)


def _fox_prompt(kaug, qt, qaug, vtaug):
    b, nb, t, kdim = kaug.shape
    tq = min(PROMPT_TQ, t)
    tk = min(FOX_TK, tq)
    return pl.pallas_call(
        functools.partial(_fox_prompt_kernel, tq=tq, tk=tk),
        grid=(b, nb, t // tq),
        in_specs=[
            pl.BlockSpec((1, 1, t, kdim), lambda bi, p, i: (bi, p, 0, 0)),
            pl.BlockSpec((1, 128, tq), lambda bi, p, i: (bi, p, i)),
            pl.BlockSpec((1, HEADS_PER_BLOCK, AUG_ROWS, tq), lambda bi, p, i: (bi, p, 0, i)),
            pl.BlockSpec((1, HEADS_PER_BLOCK, V_ROWS, t), lambda bi, p, i: (bi, p, 0, 0)),
        ],
        out_specs=pl.BlockSpec((1, tq, 128), lambda bi, p, i: (bi, i, p)),
        out_shape=jax.ShapeDtypeStruct((b, t, nb * 128), BF16),
        scratch_shapes=[pltpu.VMEM((HEADS_PER_BLOCK, kdim, tq), BF16)],
        compiler_params=_cparams("parallel", "parallel", "arbitrary"),
        name="fox_prompt",
    )(kaug, qt, qaug, vtaug)


def _sb_prompt_kernel(k_ref, qt_ref, vt_ref, ut_ref, o_ref, q_ref, *, tq, tk):
    i = pl.program_id(2)
    qt2 = qt_ref[0]
    for hh in range(HEADS_PER_BLOCK):
        q_ref[hh] = jnp.where(_head_sublane_mask(qt2.shape, hh), qt2, jnp.zeros_like(qt2))
    ut2 = ut_ref[...]
    cb = ut2.shape[0]
    nsub = tk // cb
    krow = lax.broadcasted_iota(jnp.int32, (tk, tq), 0)
    qcol = lax.broadcasted_iota(jnp.int32, (tk, tq), 1)
    sub = tq // tk

    def step(j, carry, diag):
        off = pl.multiple_of(j * tk, tk)
        kt = k_ref[0, pl.ds(off, tk), :]
        new = []
        for hh in range(HEADS_PER_BLOCK):
            rest, acc = carry[hh]
            z = _dot(kt, q_ref[hh])
            pr = jnp.maximum(z, 0.0) + jnp.log2(1.0 + jnp.exp2(-jnp.abs(z)))
            if diag is not None:
                valid = krow + diag * tk < qcol
                pr = jnp.where(valid, pr, 0.0)
            take = z - pr
            blocks = [None] * nsub
            for blk in reversed(range(nsub)):
                prb = pr[blk * cb:(blk + 1) * cb]
                hi = prb.astype(BF16)
                lo = (prb - hi.astype(F32)).astype(BF16)
                later = _dot(ut2, jnp.concatenate([hi, lo], axis=0))
                blocks[blk] = jnp.exp2(take[blk * cb:(blk + 1) * cb] + later + rest)
                rest = rest + later[0:1] - prb[0:1]
            a = blocks[0] if nsub == 1 else jnp.concatenate(blocks, axis=0)
            if diag is not None:
                a = jnp.where(valid, a, 0.0)
            vt = vt_ref[0, HEAD_DIM * hh:HEAD_DIM * (hh + 1), pl.ds(off, tk)]
            new.append((rest, acc + _dot(vt, a.astype(BF16))))
        return tuple(new)

    carry = tuple((jnp.zeros((1, tq), F32), jnp.zeros((HEAD_DIM, tq), F32)) for _ in range(HEADS_PER_BLOCK))
    for d in reversed(range(sub)):
        carry = step(i * sub + d, carry, d)
    final = lax.fori_loop(0, i * sub, lambda n, c: step(i * sub - 1 - n, c, None), carry)
    o_t = jnp.concatenate([acc for _, acc in final], axis=0)
    o_ref[0] = o_t.T.astype(o_ref.dtype)


def _sb_prompt(krows, qt, vt):
    b, t, w = krows.shape
    tq = min(PROMPT_TQ, t)
    tk = min(SB_TK, tq)
    cb = min(CUM_BLOCK, tk)
    nb = w // 128
    ut = -(jnp.arange(cb)[None, :] > jnp.arange(cb)[:, None]).astype(BF16)
    ut = jnp.concatenate([ut, ut], axis=1)
    return pl.pallas_call(
        functools.partial(_sb_prompt_kernel, tq=tq, tk=tk),
        grid=(b, nb, t // tq),
        in_specs=[
            pl.BlockSpec((1, t, 128), lambda bi, p, i: (bi, 0, p)),
            pl.BlockSpec((1, 128, tq), lambda bi, p, i: (bi, p, i)),
            pl.BlockSpec((1, 128, t), lambda bi, p, i: (bi, p, 0)),
            pl.BlockSpec((cb, 2 * cb), lambda bi, p, i: (0, 0)),
        ],
        out_specs=pl.BlockSpec((1, tq, 128), lambda bi, p, i: (bi, i, p)),
        out_shape=jax.ShapeDtypeStruct((b, t, w), BF16),
        scratch_shapes=[pltpu.VMEM((HEADS_PER_BLOCK, 128, tq), BF16)],
        compiler_params=_cparams("parallel", "parallel", "arbitrary"),
        name="sb_prompt",
    )(krows, qt, vt, ut)


def _log_gamma(head):
    pw = jnp.zeros(head.shape, F32)
    for h in range(HEADS_PER_BLOCK * (GROUP_W // 128)):
        pw = jnp.where(head == h, 2.0 ** (-5.0 - h), pw)
    return jnp.log(1.0 - pw)


def _ret_prompt_kernel(q_ref, k_ref, v_ref, o_ref, s_ref, state_ref, dec_ref, qkd_ref, *, ch):
    p = pl.program_id(1)
    c = pl.program_id(2)

    @pl.when(c == 0)
    def _():
        state_ref[...] = jnp.zeros_like(state_ref)
        ii = lax.broadcasted_iota(jnp.int32, (ch, ch), 0)
        jj = lax.broadcasted_iota(jnp.int32, (ch, ch), 1)
        diff = ii - jj
        pos = lax.broadcasted_iota(jnp.int32, (ch, 1), 0).astype(F32)
        for hh in range(HEADS_PER_BLOCK):
            lg = _log_gamma(jnp.full((1, 1), HEADS_PER_BLOCK * p + hh, jnp.int32))
            dec_ref[hh] = jnp.where(diff >= 0, jnp.exp(jnp.maximum(diff, 0).astype(F32) * lg), 0.0)
            qkd_ref[hh, :, 0:1] = jnp.exp((pos + 1.0) * lg)
            qkd_ref[hh, :, 1:2] = jnp.exp((ch - 1.0 - pos) * lg)
            qkd_ref[hh, :, 2:3] = jnp.broadcast_to(jnp.exp(ch * lg), (ch, 1))

    q2 = q_ref[0]
    k2 = k_ref[0]
    v2 = v_ref[0]
    outs = []
    for hh in range(HEADS_PER_BLOCK):
        qh = jnp.where(_head_lane_mask(q2.shape, hh), q2, jnp.zeros_like(q2))
        scores = _dot_nt(qh, k2) * dec_ref[hh]
        inner = _dot(scores.astype(BF16), v2)
        state = state_ref[hh]
        q_dec = (qh.astype(F32) * qkd_ref[hh, :, 0:1]).astype(BF16)
        cross = _dot(q_dec, state.astype(BF16))
        outs.append(inner + cross)
        k_dec = (k2.astype(F32) * qkd_ref[hh, :, 1:2]).astype(BF16)
        state_ref[hh] = qkd_ref[hh, 0:1, 2:3] * state + _dot_tn(k_dec, v2)
    o_ref[0] = jnp.where(_head_lane_mask(outs[0].shape, 0), outs[0], outs[1])

    @pl.when(c == pl.num_programs(2) - 1)
    def _():
        s_ref[0, 0] = state_ref[...]


def _ret_prompt(q, k, v):
    b, t, w = q.shape
    ch = min(256, t)
    nb = w // 128
    tile = pl.BlockSpec((1, ch, 128), lambda bi, p, c: (bi, c, p))
    return pl.pallas_call(
        functools.partial(_ret_prompt_kernel, ch=ch),
        grid=(b, nb, t // ch),
        in_specs=[tile, tile, tile],
        out_specs=[tile, pl.BlockSpec((1, 1, HEADS_PER_BLOCK, 128, 128), lambda bi, p, c: (bi, p, 0, 0, 0))],
        out_shape=[jax.ShapeDtypeStruct((b, t, w), F32),
                   jax.ShapeDtypeStruct((b, nb, HEADS_PER_BLOCK, 128, 128), F32)],
        scratch_shapes=[pltpu.VMEM((HEADS_PER_BLOCK, 128, 128), F32),
                        pltpu.VMEM((HEADS_PER_BLOCK, ch, ch), F32),
                        pltpu.VMEM((HEADS_PER_BLOCK, ch, 3), F32)],
        compiler_params=_cparams("parallel", "parallel", "arbitrary"),
        name="ret_prompt",
    )(q, k, v)


def _ret_sample_kernel(q_ref, k_ref, v_ref, s_ref, o_ref, snew_ref, *, n_new):
    q = q_ref[0]
    k = k_ref[0]
    v = v_ref[0]
    ii = lax.broadcasted_iota(jnp.int32, (n_new, n_new), 0)
    jj = lax.broadcasted_iota(jnp.int32, (n_new, n_new), 1)
    diff = ii - jj
    pos = lax.broadcasted_iota(jnp.int32, (n_new, 1), 0).astype(F32)
    o_ref[0] = jnp.zeros(o_ref.shape[1:], F32)
    for h in range(H_RET):
        lg = _log_gamma(jnp.full((1, 1), h, jnp.int32))
        sl = slice(h * HEAD_DIM, (h + 1) * HEAD_DIM)
        qh = q[:, sl]
        kh = k[:, sl]
        vh = v[:, sl].astype(BF16)
        decay = jnp.where(diff >= 0, jnp.exp(jnp.maximum(diff, 0).astype(F32) * lg), 0.0)
        scores = _dot_nt(qh.astype(BF16), kh.astype(BF16)) * decay
        inner = _dot(scores.astype(BF16), vh)
        state = s_ref[0, h]
        cross = _dot((qh * jnp.exp((pos + 1.0) * lg)).astype(BF16), state.astype(BF16))
        o_ref[0, :, sl] = inner + cross
        k_dec = (kh * jnp.exp((n_new - 1.0 - pos) * lg)).astype(BF16)
        snew_ref[0, h] = jnp.exp(n_new * lg) * state + _dot_tn(k_dec, vh)


def _ret_sample(q, k, v, state):
    nb, n_new, w = q.shape
    tok = pl.BlockSpec((1, n_new, w), lambda i: (i, 0, 0))
    st = pl.BlockSpec((1, H_RET, HEAD_DIM, HEAD_DIM), lambda i: (i, 0, 0, 0))
    return pl.pallas_call(
        functools.partial(_ret_sample_kernel, n_new=n_new),
        grid=(nb,),
        in_specs=[tok, tok, tok, st],
        out_specs=[tok, st],
        out_shape=[jax.ShapeDtypeStruct((nb, n_new, w), F32),
                   jax.ShapeDtypeStruct(state.shape, F32)],
        compiler_params=_cparams("parallel"),
        name="ret_sample",
    )(q, k, v, state)


ROWS_PER_TOKEN = 8


def _row_token(shape):
    return lax.broadcasted_iota(jnp.int32, shape, 0) // ROWS_PER_TOKEN


def _collapse_heads(acc, n_new, width):
    r = lax.broadcasted_iota(jnp.int32, acc.shape, 0) % ROWS_PER_TOKEN
    lane_head = lax.broadcasted_iota(jnp.int32, acc.shape, 1) // HEAD_DIM
    kept = jnp.where(r == lane_head, acc, 0.0)
    return jnp.sum(kept.reshape(n_new, ROWS_PER_TOKEN, width), axis=1)


def _fox_decode_kernel(pt_ref, qbd_ref, knew_ref, vnew_ref, lfcol_ref, lfmat_ref, lfc_ref, u_ref, *rest,
                       n_pp, n_new, n_pages):
    k_refs = rest[:n_pp]
    v_refs = rest[n_pp:2 * n_pp]
    o_ref = rest[2 * n_pp]
    m_ref, l_ref, acc_ref, suf_ref, lf_ref = rest[2 * n_pp + 1:]
    b = pl.program_id(0)
    g = pl.program_id(1)
    nrow = n_new * ROWS_PER_TOKEN
    qbd = qbd_ref[0]
    u = u_ref[...]

    lfcol = lfcol_ref[0]
    cnew_col = jnp.concatenate(
        [sum(lfcol[j * ROWS_PER_TOKEN:(j + 1) * ROWS_PER_TOKEN] for j in range(t + 1)) for t in range(n_new)], axis=0)

    def attend(s, pv):
        m_new = jnp.maximum(m_ref[...], jnp.max(s, axis=1, keepdims=True))
        alpha = jnp.exp(m_ref[...] - m_new)
        p = jnp.exp(s - m_new)
        l_ref[...] = alpha * l_ref[...] + jnp.sum(p, axis=1, keepdims=True)
        acc_ref[...] = alpha * acc_ref[...] + pv(p.astype(BF16))
        m_ref[...] = m_new

    @pl.when(g == 0)
    def _():
        m_ref[...] = jnp.full_like(m_ref, NEG_INF)
        l_ref[...] = jnp.zeros_like(l_ref)
        acc_ref[...] = jnp.zeros_like(acc_ref)
        suf_ref[...] = jnp.zeros_like(suf_ref)
        lf_ref[...] = jnp.zeros_like(lf_ref)
        lane = lax.broadcasted_iota(jnp.int32, (nrow, 128), 1)
        lfmat = lfmat_ref[0]
        cnew_mat = jnp.zeros_like(lfmat)
        for j in range(n_new):
            cnew_mat = cnew_mat + jnp.where(lane >= j, lfmat[:, j:j + 1], 0.0)
        s = _dot(qbd, knew_ref[0]) + (cnew_col - cnew_mat)
        s = jnp.where(lane <= _row_token((nrow, 128)), s, NEG_INF)
        attend(s, lambda p: _dot(p, vnew_ref[0]))

    for ii in range(n_pp):
        page = pt_ref[b, n_pages - 1 - (g * n_pp + ii)]
        for h in range(H_FOX):
            lf_ref[pl.ds(ii * ROWS_PER_TOKEN + h, 1), :] = lfc_ref[0, h, pl.ds(page, 1), :]
    lf_all = lf_ref[...]
    within = _dot_acc3(lf_all, u)
    totals = jnp.sum(lf_all, axis=1, keepdims=True)
    later_pages = suf_ref[...]
    scores = []
    for ii in range(n_pp):
        sl = slice(ii * ROWS_PER_TOKEN, (ii + 1) * ROWS_PER_TOKEN)
        bias = jnp.concatenate([within[sl] + later_pages] * n_new, axis=0) + cnew_col
        kp = k_refs[ii][0, 0].reshape(H_FOX * HEAD_DIM, 128).astype(BF16)
        scores.append(_dot(qbd, kp) + bias)
        later_pages = later_pages + totals[sl]
    suf_ref[...] = later_pages

    def pv(p):
        out = None
        for ii in range(n_pp):
            vp = v_refs[ii][0, 0].reshape(H_FOX * HEAD_DIM, 128).astype(BF16)
            part = _dot_nt(p[:, ii * 128:(ii + 1) * 128], vp)
            out = part if out is None else out + part
        return out

    attend(jnp.concatenate(scores, axis=1), pv)

    @pl.when(g == pl.num_programs(1) - 1)
    def _():
        o_ref[0] = _collapse_heads(acc_ref[...] / l_ref[...], n_new, GROUP_W)


def _pages_per_step(n_pages):
    return min(16, n_pages)


def _fox_decode(layer, page_table, qbd, knew_t, vnew, lfcol, lfmat, cache_k, cache_v, cache_lf):
    nb, nrow, w = qbd.shape
    n_new = nrow // ROWS_PER_TOKEN
    n_pages = page_table.shape[1]
    n_pp = _pages_per_step(n_pages)
    u = (jnp.arange(128)[:, None] > jnp.arange(128)[None, :]).astype(BF16)
    per_seq = lambda a: pl.BlockSpec((1,) + a.shape[1:], lambda b, g, pt: (b,) + (0,) * (a.ndim - 1))

    def page_spec(ii):
        return pl.BlockSpec((1, 1) + cache_k.shape[2:],
                            lambda b, g, pt: (layer, pt[b, n_pages - 1 - (g * n_pp + ii)], 0, 0, 0))

    grid_spec = pltpu.PrefetchScalarGridSpec(
        num_scalar_prefetch=1,
        grid=(nb, n_pages // n_pp),
        in_specs=[per_seq(qbd), per_seq(knew_t), per_seq(vnew), per_seq(lfcol), per_seq(lfmat),
                  pl.BlockSpec((1,) + cache_lf.shape[1:], lambda b, g, pt: (layer, 0, 0, 0)),
                  pl.BlockSpec(u.shape, lambda b, g, pt: (0, 0))]
        + [page_spec(ii) for ii in range(n_pp)] * 2,
        out_specs=pl.BlockSpec((1, n_new, w), lambda b, g, pt: (b, 0, 0)),
        scratch_shapes=[pltpu.VMEM((nrow, 1), F32), pltpu.VMEM((nrow, 1), F32), pltpu.VMEM((nrow, w), F32),
                        pltpu.VMEM((ROWS_PER_TOKEN, 1), F32), pltpu.VMEM((n_pp * ROWS_PER_TOKEN, 128), F32)],
    )
    return pl.pallas_call(
        functools.partial(_fox_decode_kernel, n_pp=n_pp, n_new=n_new, n_pages=n_pages),
        grid_spec=grid_spec,
        out_shape=jax.ShapeDtypeStruct((nb, n_new, w), F32),
        compiler_params=_cparams("parallel", "arbitrary"),
        name="fox_decode",
    )(page_table, qbd, knew_t, vnew, lfcol, lfmat, cache_lf, u, *([cache_k] * n_pp), *([cache_v] * n_pp))


def _sb_decode_kernel(pt_ref, qbd_ref, knew_ref, vnew_ref, u_ref, *rest, n_pp, n_new):
    k_refs = rest[:n_pp]
    v_refs = rest[n_pp:2 * n_pp]
    o_ref = rest[2 * n_pp]
    rest_ref, acc_ref = rest[2 * n_pp + 1:]
    g = pl.program_id(1)
    nrow = n_new * ROWS_PER_TOKEN
    width = H_SB * HEAD_DIM
    qbd = qbd_ref[0]
    u2 = u_ref[...]

    def weights(z_blocks, valid):
        n = len(z_blocks)
        z = jnp.concatenate(z_blocks, axis=0)
        pr = jnp.maximum(z, 0.0) + jnp.log2(1.0 + jnp.exp2(-jnp.abs(z)))
        if valid is not None:
            pr = jnp.where(valid, pr, 0.0)
        hi = pr.astype(BF16)
        lo = (pr - hi.astype(F32)).astype(BF16)
        later = _dot(jnp.concatenate([hi, lo], axis=1), u2)
        run = rest_ref[...]
        out = []
        for ii in range(n):
            sl = slice(ii * nrow, (ii + 1) * nrow)
            a = jnp.exp2(z[sl] - pr[sl] + later[sl] + run)
            if valid is not None:
                a = jnp.where(valid, a, 0.0)
            out.append(a.astype(BF16))
            run = run + later[sl, 0:1] - pr[sl, 0:1]
        rest_ref[...] = run
        return out

    @pl.when(g == 0)
    def _():
        rest_ref[...] = jnp.zeros_like(rest_ref)
        acc_ref[...] = jnp.zeros_like(acc_ref)
        lane = lax.broadcasted_iota(jnp.int32, (nrow, 128), 1)
        (a,) = weights([_dot(qbd, knew_ref[0])], lane < _row_token((nrow, 128)))
        acc_ref[...] = _dot(a, vnew_ref[0])

    z_blocks = [_dot(qbd, k_refs[ii][0, 0].reshape(width, 128).astype(BF16)) for ii in range(n_pp)]
    a_blocks = weights(z_blocks, None)
    out = acc_ref[...]
    for ii in range(n_pp):
        out = out + _dot_nt(a_blocks[ii], v_refs[ii][0, 0].reshape(width, 128).astype(BF16))
    acc_ref[...] = out

    @pl.when(g == pl.num_programs(1) - 1)
    def _():
        o_ref[0] = _collapse_heads(acc_ref[...], n_new, width)


def _sb_decode(layer, page_table, qbd, knew_t, vnew, cache_k, cache_v):
    nb, nrow, width = qbd.shape
    n_new = nrow // ROWS_PER_TOKEN
    n_pages = page_table.shape[1]
    n_pp = _pages_per_step(n_pages)
    u = -(jnp.arange(128)[:, None] > jnp.arange(128)[None, :]).astype(BF16)
    u = jnp.concatenate([u, u], axis=0)
    per_seq = lambda a: pl.BlockSpec((1,) + a.shape[1:], lambda b, g, pt: (b,) + (0,) * (a.ndim - 1))

    def page_spec(ii):
        return pl.BlockSpec((1, 1) + cache_k.shape[2:],
                            lambda b, g, pt: (layer, pt[b, n_pages - 1 - (g * n_pp + ii)], 0, 0, 0))

    grid_spec = pltpu.PrefetchScalarGridSpec(
        num_scalar_prefetch=1,
        grid=(nb, n_pages // n_pp),
        in_specs=[per_seq(qbd), per_seq(knew_t), per_seq(vnew), pl.BlockSpec(u.shape, lambda b, g, pt: (0, 0))]
        + [page_spec(ii) for ii in range(n_pp)] * 2,
        out_specs=pl.BlockSpec((1, n_new, width), lambda b, g, pt: (b, 0, 0)),
        scratch_shapes=[pltpu.VMEM((nrow, 1), F32), pltpu.VMEM((nrow, width), F32)],
    )
    return pl.pallas_call(
        functools.partial(_sb_decode_kernel, n_pp=n_pp, n_new=n_new),
        grid_spec=grid_spec,
        out_shape=jax.ShapeDtypeStruct((nb, n_new, width), F32),
        compiler_params=_cparams("parallel", "arbitrary"),
        name="sb_decode",
    )(page_table, qbd, knew_t, vnew, u, *([cache_k] * n_pp), *([cache_v] * n_pp))


def _pad_cols(a, width=GROUP_W):
    return jnp.pad(a, ((0, 0), (0, width - a.shape[1])))


def _split_mix_weight(w):
    wf, wr, ws = H_FOX * HEAD_DIM, H_RET * HEAD_DIM, H_SB * HEAD_DIM
    sizes = (wf, wf, wf, H_FOX, wr, wr, wr, wr, ws, ws, ws)
    offs = np.cumsum((0,) + sizes)
    return [w[:, offs[i]:offs[i + 1]] for i in range(len(sizes))]


def _rope_swap_perm(n_heads):
    idx = np.arange(n_heads * HEAD_DIM)
    return (idx // HEAD_DIM) * HEAD_DIM + (idx % HEAD_DIM + HEAD_DIM // 2) % HEAD_DIM


def _rope_tables(pos, n_heads):
    half = HEAD_DIM // 2
    inv_freq = ROPE_BASE ** (-jnp.arange(half, dtype=F32) / half)
    ang = pos.astype(F32)[:, None] * inv_freq[None, :]
    cos, sin = jnp.cos(ang), jnp.sin(ang)
    cos_t = jnp.tile(jnp.concatenate([cos, cos], axis=1), (1, n_heads))
    sin_t = jnp.tile(jnp.concatenate([-sin, sin], axis=1), (1, n_heads))
    return _pad_cols(cos_t), _pad_cols(sin_t)


def _block_diag_queries(q, n_heads):
    nb, n_new, w = q.shape
    slot = jnp.arange(ROWS_PER_TOKEN)[:, None]
    lane_head = (jnp.arange(w) // HEAD_DIM)[None, :]
    mask = (slot == lane_head) & (slot < n_heads)
    qb = jnp.where(mask[None, None], q[:, :, None, :], 0.0)
    return qb.reshape(nb, n_new * ROWS_PER_TOKEN, w).astype(BF16)


def _new_keys_t(k, width):
    nb, n_new, _ = k.shape
    kt = jnp.swapaxes(k[:, :, :width], 1, 2)
    return jnp.pad(kt, ((0, 0), (0, 0), (0, 128 - n_new))).astype(BF16)


def _new_values(v, width):
    nb, n_new, _ = v.shape
    return jnp.pad(v[:, :, :width], ((0, 0), (0, 128 - n_new), (0, 0))).astype(BF16)


def kernel(x_prompt, x_sample, cache_fox_k, cache_fox_v, cache_fox_logf, cache_sb_k, cache_sb_v, state_ret,
           page_table, norm_g, w_ffn1_in, w_ffn1_out, w_mix_in, b_forget, ret_gn_g, w_mix_out, w_ffn2_in,
           w_ffn2_out):
    depth = norm_g.shape[0]
    bsz, seq, d = x_prompt.shape
    nb, n_new, _ = x_sample.shape
    n_pages, page_size = page_table.shape[1], cache_fox_k.shape[2]
    past = n_pages * page_size
    wf, wr, ws = H_FOX * HEAD_DIM, H_RET * HEAD_DIM, H_SB * HEAD_DIM
    w = GROUP_W

    ck_fox = jnp.transpose(cache_fox_k, (0, 1, 3, 4, 2))
    cv_fox = jnp.transpose(cache_fox_v, (0, 1, 3, 4, 2))
    ck_sb = jnp.transpose(cache_sb_k, (0, 1, 3, 4, 2))
    cv_sb = jnp.transpose(cache_sb_v, (0, 1, 3, 4, 2))
    clf = jnp.transpose(cache_fox_logf, (0, 3, 1, 2))

    cos_p, sin_p = _rope_tables(jnp.arange(seq, dtype=jnp.int32), H_RET)
    cos_s, sin_s = _rope_tables(past + jnp.arange(n_new, dtype=jnp.int32), H_RET)
    cos_s, sin_s = jnp.tile(cos_s, (nb, 1)), jnp.tile(sin_s, (nb, 1))
    perm = _rope_swap_perm(H_RET)
    avg = jnp.asarray(np.kron(np.eye(w // HEAD_DIM), np.full((HEAD_DIM, HEAD_DIM), 1.0 / HEAD_DIM)), BF16)

    yp = x_prompt.reshape(bsz * seq, d)
    ys = x_sample.reshape(nb * n_new, d)
    st_p, st_s = [], []
    for l in range(depth):
        g = [norm_g[l, i][None, :] for i in range(norm_g.shape[1])]
        w1_in, w1_out = w_ffn1_in[l].astype(BF16), w_ffn1_out[l].astype(BF16)
        w2_in, w2_out = w_ffn2_in[l].astype(BF16), w_ffn2_out[l].astype(BF16)
        qa, ka, va, fa, qr, kr, vr, gr, qs, ks, vs = _split_mix_weight(w_mix_in[l])
        ret_cols = [_pad_cols(qr), _pad_cols(qr[:, perm]), _pad_cols(kr), _pad_cols(kr[:, perm]),
                    _pad_cols(vr), _pad_cols(gr)]
        kv_cols = [ka, va, _pad_cols(ks), _pad_cols(vs)]
        wrow = jnp.concatenate([ka, _pad_cols(ks)] + ret_cols, axis=1).astype(BF16)
        wt = jnp.concatenate(kv_cols + [qa, _pad_cols(qs), _pad_cols(fa, 16)], axis=1).T.astype(BF16)
        w_samp = jnp.concatenate([qa, _pad_cols(qs)] + ret_cols + kv_cols + [_pad_cols(fa, 128)], axis=1).astype(BF16)
        bf_col = jnp.pad(b_forget[l], (0, 16 - H_FOX))[:, None]
        bf_row = jnp.pad(b_forget[l], (0, 128 - H_FOX))[None, :]
        gn = _pad_cols(ret_gn_g[l][None, :])
        wo = w_mix_out[l]
        wo1 = wo[:wf].astype(BF16)
        wo2 = jnp.pad(wo[wf:wf + wr], ((0, w - wr), (0, 0))).astype(BF16)
        wo3 = jnp.pad(wo[wf + wr:], ((0, w - ws), (0, 0))).astype(BF16)

        hp = _ffn_block(yp, g[0], w1_in, w1_out, g[1])
        (ksr, qr_p, kr_p, vr_p, gr_p, kaug, kat, vat, kst, vst, vstb, qat, qst, lft, qaug, vtaug) = _proj_prompt(
            hp.reshape(bsz, seq, d), g[2], wrow, wt, bf_col, cos_p, sin_p)
        o_fox = _fox_prompt(kaug, qat, qaug, vtaug)
        o_ret, s_raw = _ret_prompt(qr_p, kr_p, vr_p)
        o_sb = _sb_prompt(ksr, qst, vstb)
        hp = _merge_out(hp, o_fox.reshape(bsz * seq, w), o_ret.reshape(bsz * seq, w), gr_p.reshape(bsz * seq, w),
                        o_sb.reshape(bsz * seq, w), gn, avg, wo1, wo2, wo3, g[3])
        yp = _ffn_block(hp, g[4], w2_in, w2_out, g[5])
        s_fin = jnp.stack([s_raw[:, h // 2, h % 2, (h % 2) * HEAD_DIM:(h % 2 + 1) * HEAD_DIM,
                                 (h % 2) * HEAD_DIM:(h % 2 + 1) * HEAD_DIM] for h in range(H_RET)], axis=1)
        to_heads = lambda a, nh: jnp.transpose(a.reshape(bsz, nh, HEAD_DIM, seq), (0, 3, 1, 2))
        st_p.append((to_heads(kat, H_FOX), to_heads(vat, H_FOX), jnp.transpose(lft[:, :H_FOX], (0, 2, 1)),
                     to_heads(kst, H_SB), to_heads(vst, H_SB), s_fin))

        hs = _ffn_block(ys, g[0], w1_in, w1_out, g[1])
        (qa_s, qs_s, qr_s, kr_s, vr_s, gr_s, ka_s, va_s, ks_s, vs_s, lf_s) = _proj_sample(
            hs, g[2], w_samp, bf_row, cos_s, sin_s)
        seqs = lambda a: a.reshape(nb, n_new, a.shape[-1])
        lf_new = seqs(lf_s)[:, :, :ROWS_PER_TOKEN]
        lfcol = lf_new.reshape(nb, n_new * ROWS_PER_TOKEN, 1)
        lfmat = jnp.broadcast_to(jnp.transpose(lf_new, (0, 2, 1))[:, None], (nb, n_new, ROWS_PER_TOKEN, n_new))
        lfmat = jnp.pad(lfmat.reshape(nb, n_new * ROWS_PER_TOKEN, n_new), ((0, 0), (0, 0), (0, 128 - n_new)))
        o_fox_s = _fox_decode(l, page_table, _block_diag_queries(seqs(qa_s), H_FOX), _new_keys_t(seqs(ka_s), wf),
                              _new_values(seqs(va_s), wf), lfcol, lfmat, ck_fox, cv_fox, clf)
        o_sb_s = _sb_decode(l, page_table, _block_diag_queries(seqs(qs_s)[:, :, :ws] * LOG2E, H_SB),
                            _new_keys_t(seqs(ks_s), ws), _new_values(seqs(vs_s), ws), ck_sb, cv_sb)
        o_ret_s, s_new = _ret_sample(seqs(qr_s), seqs(kr_s), seqs(vr_s), state_ret[l])
        hs = _merge_out(hs, o_fox_s.reshape(nb * n_new, w).astype(BF16), o_ret_s.reshape(nb * n_new, w), gr_s,
                        _pad_cols(o_sb_s.reshape(nb * n_new, ws)).astype(BF16), gn, avg, wo1, wo2, wo3, g[3])
        ys = _ffn_block(hs, g[4], w2_in, w2_out, g[5])
        heads = lambda a, nh: a[:, :nh * HEAD_DIM].reshape(nb, n_new, nh, HEAD_DIM)
        st_s.append((heads(ka_s, H_FOX), heads(va_s, H_FOX), seqs(lf_s)[:, :, :H_FOX],
                     heads(ks_s, H_SB), heads(vs_s, H_SB), s_new))

    stk = lambda sts, i: jnp.stack([s[i] for s in sts], axis=0)
    return (yp.reshape(bsz, seq, d), ys.reshape(nb, n_new, d),
            stk(st_p, 0), stk(st_p, 1), stk(st_p, 2), stk(st_p, 3), stk(st_p, 4), stk(st_p, 5),
            stk(st_s, 0), stk(st_s, 1), stk(st_s, 2), stk(st_s, 3), stk(st_s, 4), stk(st_s, 5))
```

```python
import functools
import math

import numpy as np
import jax
import jax.numpy as jnp
from jax import lax
from jax.experimental import pallas as pl
from jax.experimental.pallas import tpu as pltpu

HEAD_DIM = 64
H_FOX = 6
H_RET = 5
H_SB = 5
GROUP_W = 384
HEADS_PER_BLOCK = 2
EPS = 1e-6
ROPE_BASE = 10000.0
Q_SCALE = HEAD_DIM ** -0.5
LOG2E = math.log2(math.e)
AUG_ROWS = 16
V_ROWS = 80
CUM_BLOCK = 256
EXP2_CLAMP = 64.0
PROMPT_TQ = 1024
FOX_TK = 1024
SB_TK = 1024
VMEM_LIMIT_BYTES = 48 * 1024 * 1024
F32 = jnp.float32
BF16 = jnp.bfloat16
NEG_INF = float("-inf")


def _cparams(*sem):
    return pltpu.CompilerParams(dimension_semantics=sem, vmem_limit_bytes=VMEM_LIMIT_BYTES)


def _dot(a, b):
    return jnp.dot(a, b, preferred_element_type=F32)


def _dot_nt(a, b):
    return lax.dot_general(a, b, (((1,), (1,)), ((), ())), preferred_element_type=F32)


def _dot_tn(a, b):
    return lax.dot_general(a, b, (((0,), (0,)), ((), ())), preferred_element_type=F32)


def _split3(x):
    hi = x.astype(BF16)
    r = x - hi.astype(F32)
    mid = r.astype(BF16)
    lo = (r - mid.astype(F32)).astype(BF16)
    return hi, mid, lo


def _dot_acc3(x, m):
    hi, mid, lo = _split3(x)
    return _dot(hi, m) + _dot(mid, m) + _dot(lo, m)


def _rms(x, g):
    return x * lax.rsqrt(jnp.mean(x * x, axis=-1, keepdims=True) + EPS) * g


def _softplus_parts(z):
    lp = jnp.log1p(jnp.exp(-jnp.abs(z)))
    return -jnp.maximum(z, 0.0) - lp, jnp.minimum(z, 0.0) - lp


def _ffn_kernel(x_ref, gpre_ref, wg_ref, wu_ref, wo_ref, gpost_ref, o_ref, xn_ref, acc_ref):
    j = pl.program_id(1)

    @pl.when(j == 0)
    def _():
        xn_ref[...] = _rms(x_ref[...], gpre_ref[...]).astype(BF16)
        acc_ref[...] = jnp.zeros_like(acc_ref)

    xn = xn_ref[...]
    gate = _dot(xn, wg_ref[...])
    up = _dot(xn, wu_ref[...])
    hidden = (gate * jax.nn.sigmoid(gate) * up).astype(BF16)
    acc_ref[...] += _dot(hidden, wo_ref[...])

    @pl.when(j == pl.num_programs(1) - 1)
    def _():
        o_ref[...] = x_ref[...] + 0.5 * _rms(acc_ref[...], gpost_ref[...])


def _ffn_block(x, g_pre, w_in, w_out, g_post):
    m, d = x.shape
    dff = w_out.shape[0]
    tm = min(512, m)
    nf = 2
    tf = dff // nf
    return pl.pallas_call(
        _ffn_kernel,
        grid=(m // tm, nf),
        in_specs=[
            pl.BlockSpec((tm, d), lambda i, j: (i, 0)),
            pl.BlockSpec((1, d), lambda i, j: (0, 0)),
            pl.BlockSpec((d, tf), lambda i, j: (0, j)),
            pl.BlockSpec((d, tf), lambda i, j: (0, nf + j)),
            pl.BlockSpec((tf, d), lambda i, j: (j, 0)),
            pl.BlockSpec((1, d), lambda i, j: (0, 0)),
        ],
        out_specs=pl.BlockSpec((tm, d), lambda i, j: (i, 0)),
        out_shape=jax.ShapeDtypeStruct((m, d), F32),
        scratch_shapes=[pltpu.VMEM((tm, d), BF16), pltpu.VMEM((tm, d), F32)],
        compiler_params=_cparams("parallel", "arbitrary"),
        name="ffn_block",
    )(x, g_pre, w_in, w_in, w_out, g_post)


def _proj_prompt_kernel(x_ref, g_ref, wrow_ref, wt_ref, bf_ref, cos_ref, sin_ref, tri_ref, kplace_ref, qplace_ref,
                        ksr_ref, qr_ref, kr_ref, vr_ref, gr_ref, kaug_ref,
                        kat_ref, vat_ref, kst_ref, vst_ref,
                        vstb_ref, qat_ref, qst_ref,
                        lf_ref, qaug_ref, vtaug_ref, carry_ref):
    w = GROUP_W
    t = pl.program_id(1)

    @pl.when(t == 0)
    def _():
        carry_ref[...] = jnp.zeros_like(carry_ref)

    xn = _rms(x_ref[0], g_ref[...]).astype(BF16)
    pr = _dot(xn, wrow_ref[...])
    cos = cos_ref[...]
    sin = sin_ref[...]
    ksr_ref[0] = pr[:, w:2 * w].astype(BF16)
    qr_ref[0] = (pr[:, 2 * w:3 * w] * cos + pr[:, 3 * w:4 * w] * sin).astype(BF16)
    kr_ref[0] = ((pr[:, 4 * w:5 * w] * cos + pr[:, 5 * w:6 * w] * sin) * Q_SCALE).astype(BF16)
    vr_ref[0] = pr[:, 6 * w:7 * w].astype(BF16)
    gr_ref[0] = pr[:, 7 * w:8 * w]

    pt = _dot_nt(wt_ref[...], xn)
    kat = pt[0:w]
    vat = pt[w:2 * w]
    kst = pt[2 * w:3 * w]
    vst = pt[3 * w:4 * w]
    kat_ref[0] = kat
    vat_ref[0] = vat
    kst_ref[0] = kst[0:H_SB * HEAD_DIM]
    vst_ref[0] = vst[0:H_SB * HEAD_DIM]
    vstb_ref[0] = vst.astype(BF16)
    qat_ref[0] = (pt[4 * w:5 * w] * (Q_SCALE * LOG2E)).astype(BF16)
    qst_ref[0] = (pt[5 * w:6 * w] * (Q_SCALE * LOG2E)).astype(BF16)

    logf = jax.nn.log_sigmoid(pt[6 * w:6 * w + 16] + bf_ref[...])
    lf_ref[0] = logf
    c = _dot_acc3(logf, tri_ref[...]) + carry_ref[...]
    carry_ref[...] = c[:, -1:]

    tm = c.shape[1]
    one_row = jnp.concatenate([jnp.ones((1, tm), BF16), jnp.zeros((15, tm), BF16)], axis=0)
    csplit = jnp.concatenate(list(_split3(c * LOG2E)) + [one_row], axis=0)
    for p in range(w // 128):
        kaug_ref[0, p, :, 0:128] = pr[:, p * 128:(p + 1) * 128].astype(BF16)
        kaug_ref[0, p, :, 128:256] = _dot_tn(csplit, kplace_ref[p]).astype(BF16)
    q_rows = _dot(qplace_ref[...], csplit).astype(BF16)
    for h in range(H_FOX):
        qaug_ref[0, h] = q_rows[h * AUG_ROWS:(h + 1) * AUG_ROWS]
        vtaug_ref[0, h, 0:HEAD_DIM, :] = vat[h * HEAD_DIM:(h + 1) * HEAD_DIM].astype(BF16)
        vtaug_ref[0, h, HEAD_DIM:V_ROWS, :] = one_row


def _fox_placements():
    nb = GROUP_W // 128
    kplace = np.zeros((nb, 64, 128), np.float32)
    qplace = np.zeros((H_FOX * AUG_ROWS, 64), np.float32)
    for h in range(H_FOX):
        p, hh = divmod(h, HEADS_PER_BLOCK)
        for sp in range(3):
            kplace[p, sp * 16 + h, hh * 3 + sp] = -1.0
            kplace[p, 48, HEADS_PER_BLOCK * 3 + sp] = 1.0
            qplace[h * AUG_ROWS + hh * 3 + sp, 48] = 1.0
            qplace[h * AUG_ROWS + HEADS_PER_BLOCK * 3 + sp, sp * 16 + h] = 1.0
    return jnp.asarray(kplace, BF16), jnp.asarray(qplace, BF16)


def _proj_prompt(x, g, wrow, wt, bf, cos, sin):
    b, t, d = x.shape
    tm = min(512, t)
    w = GROUP_W
    nb = w // 128
    tri = (jnp.arange(tm)[:, None] <= jnp.arange(tm)[None, :]).astype(BF16)
    kplace, qplace = _fox_placements()
    row = lambda dt: jax.ShapeDtypeStruct((b, t, w), dt)
    col = lambda n, dt: jax.ShapeDtypeStruct((b, n, t), dt)
    rspec = pl.BlockSpec((1, tm, w), lambda bi, ti: (bi, ti, 0))
    cspec = lambda n: pl.BlockSpec((1, n, tm), lambda bi, ti: (bi, 0, ti))
    const = lambda a: pl.BlockSpec(a.shape, lambda bi, ti: (0,) * a.ndim)
    nsb = H_SB * HEAD_DIM
    return pl.pallas_call(
        _proj_prompt_kernel,
        grid=(b, t // tm),
        in_specs=[
            pl.BlockSpec((1, tm, d), lambda bi, ti: (bi, ti, 0)),
            const(g), const(wrow), const(wt), const(bf),
            pl.BlockSpec((tm, w), lambda bi, ti: (ti, 0)),
            pl.BlockSpec((tm, w), lambda bi, ti: (ti, 0)),
            const(tri), const(kplace), const(qplace),
        ],
        out_specs=[rspec] * 5 + [pl.BlockSpec((1, nb, tm, 256), lambda bi, ti: (bi, 0, ti, 0))]
        + [cspec(w), cspec(w), cspec(nsb), cspec(nsb)] + [cspec(w)] * 3 + [cspec(16)]
        + [pl.BlockSpec((1, H_FOX, AUG_ROWS, tm), lambda bi, ti: (bi, 0, 0, ti)),
           pl.BlockSpec((1, H_FOX, V_ROWS, tm), lambda bi, ti: (bi, 0, 0, ti))],
        out_shape=[row(BF16)] * 4 + [row(F32), jax.ShapeDtypeStruct((b, nb, t, 256), BF16)]
        + [col(w, F32), col(w, F32), col(nsb, F32), col(nsb, F32)] + [col(w, BF16)] * 3 + [col(16, F32)]
        + [jax.ShapeDtypeStruct((b, H_FOX, AUG_ROWS, t), BF16), jax.ShapeDtypeStruct((b, H_FOX, V_ROWS, t), BF16)],
        scratch_shapes=[pltpu.VMEM((16, 1), F32)],
        compiler_params=_cparams("parallel", "arbitrary"),
        name="proj_prompt",
    )(x, g, wrow, wt, bf, cos, sin, tri, kplace, qplace)


def _proj_sample_kernel(x_ref, g_ref, w_ref, bf_ref, cos_ref, sin_ref,
                        qa_ref, qs_ref, qr_ref, kr_ref, vr_ref, gr_ref,
                        ka_ref, va_ref, ks_ref, vs_ref, lf_ref):
    w = GROUP_W
    xn = _rms(x_ref[...], g_ref[...]).astype(BF16)
    pr = _dot(xn, w_ref[...])
    cos = cos_ref[...]
    sin = sin_ref[...]
    qa_ref[...] = pr[:, 0:w] * Q_SCALE
    qs_ref[...] = pr[:, w:2 * w] * Q_SCALE
    qr_ref[...] = pr[:, 2 * w:3 * w] * cos + pr[:, 3 * w:4 * w] * sin
    kr_ref[...] = (pr[:, 4 * w:5 * w] * cos + pr[:, 5 * w:6 * w] * sin) * Q_SCALE
    vr_ref[...] = pr[:, 6 * w:7 * w]
    gr_ref[...] = pr[:, 7 * w:8 * w]
    ka_ref[...] = pr[:, 8 * w:9 * w]
    va_ref[...] = pr[:, 9 * w:10 * w]
    ks_ref[...] = pr[:, 10 * w:11 * w]
    vs_ref[...] = pr[:, 11 * w:12 * w]
    lf_ref[...] = jax.nn.log_sigmoid(pr[:, 12 * w:12 * w + 128] + bf_ref[...])


def _proj_sample(x, g, w_all, bf_row, cos, sin):
    m = x.shape[0]
    w = GROUP_W
    return pl.pallas_call(
        _proj_sample_kernel,
        out_shape=[jax.ShapeDtypeStruct((m, w), F32)] * 10 + [jax.ShapeDtypeStruct((m, 128), F32)],
        compiler_params=pltpu.CompilerParams(vmem_limit_bytes=VMEM_LIMIT_BYTES),
        name="proj_sample",
    )(x, g, w_all, bf_row, cos, sin)


def _merge_kernel(h_ref, of_ref, oret_ref, gr_ref, osba_ref, osbb_ref, gn_ref, avg_ref,
                  w1_ref, w2_ref, w3a_ref, w3b_ref, g_ref, o_ref):
    x = oret_ref[...]
    avg = avg_ref[...]
    hi, mid, lo = _split3(x)
    mu = _dot(hi, avg) + _dot(mid, avg) + _dot(lo, avg)
    dev = x - mu
    var = _dot_acc3(dev * dev, avg)
    gate = gr_ref[...]
    r = dev * lax.rsqrt(var + EPS) * gn_ref[...] * (gate * jax.nn.sigmoid(gate))
    y = (_dot(of_ref[...], w1_ref[...]) + _dot(r.astype(BF16), w2_ref[...])
         + _dot(osba_ref[...], w3a_ref[...]) + _dot(osbb_ref[...], w3b_ref[...]))
    o_ref[...] = h_ref[...] + _rms(y, g_ref[...])


def _merge_out(h, o_fox, o_ret, gr, o_sb_a, o_sb_b, gn, avg, w1, w2, w3, g):
    m, d = h.shape
    w = GROUP_W
    wa = o_sb_a.shape[1]
    w3a, w3b = w3[:wa], w3[wa:]
    tm = min(512, m)
    rows = lambda n: pl.BlockSpec((tm, n), lambda i: (i, 0))
    const = lambda a: pl.BlockSpec(a.shape, lambda i: (0,) * a.ndim)
    return pl.pallas_call(
        _merge_kernel,
        grid=(m // tm,),
        in_specs=[rows(d), rows(w), rows(w), rows(w), rows(wa), rows(w - wa), const(gn), const(avg),
                  const(w1), const(w2), const(w3a), const(w3b), const(g)],
        out_specs=rows(d),
        out_shape=jax.ShapeDtypeStruct((m, d), F32),
        compiler_params=_cparams("parallel"),
        name="merge_out",
    )(h, o_fox, o_ret, gr, o_sb_a, o_sb_b, gn, avg, w1, w2, w3a, w3b, g)


def _head_lane_mask(shape, hh):
    lane = lax.broadcasted_iota(jnp.int32, shape, len(shape) - 1)
    return (lane >= HEAD_DIM * hh) & (lane < HEAD_DIM * (hh + 1))


def _head_sublane_mask(shape, hh):
    r = lax.broadcasted_iota(jnp.int32, shape, 0)
    return (r >= HEAD_DIM * hh) & (r < HEAD_DIM * (hh + 1))


def _fox_prompt_kernel(kaug_ref, qt_ref, qaug_ref, vt_ref, o_ref, qa_ref, *, tq, tk):
    i = pl.program_id(2)
    qt2 = qt_ref[0]
    kdim = kaug_ref.shape[-1]
    zpad = jnp.zeros((kdim - 128 - AUG_ROWS, tq), BF16)
    for hh in range(HEADS_PER_BLOCK):
        qa_ref[hh] = jnp.concatenate(
            [jnp.where(_head_sublane_mask(qt2.shape, hh), qt2, jnp.zeros_like(qt2)), qaug_ref[0, hh], zpad], axis=0)
    krow = lax.broadcasted_iota(jnp.int32, (tk, tq), 0)
    qcol = lax.broadcasted_iota(jnp.int32, (tk, tq), 1)
    sub = tq // tk

    def step(j, carry, diag):
        off = pl.multiple_of(j * tk, tk)
        kt = kaug_ref[0, 0, pl.ds(off, tk), :]
        new = []
        for hh in range(HEADS_PER_BLOCK):
            m, acc = carry[hh]
            s = _dot(kt, qa_ref[hh])
            if diag is not None:
                s = jnp.where(krow + diag * tk <= qcol, s, NEG_INF)
            m_new = jnp.maximum(m, jnp.max(s, axis=0, keepdims=True))
            alpha = jnp.exp2(m - m_new)
            p = jnp.exp2(s - m_new).astype(BF16)
            vt = vt_ref[0, hh, :, pl.ds(off, tk)]
            new.append((m_new, alpha * acc + _dot(vt, p)))
        return tuple(new)

    init = tuple((jnp.full((1, tq), NEG_INF, F32), jnp.zeros((V_ROWS, tq), F32)) for _ in range(HEADS_PER_BLOCK))
    carry = lax.fori_loop(0, i * sub, functools.partial(step, diag=None), init)
    for d in range(sub):
        carry = step(i * sub + d, carry, d)
    o_t = jnp.concatenate([acc[0:HEAD_DIM] / acc[HEAD_DIM:HEAD_DIM + 1] for _, acc in carry], axis=0)
    o_ref[0] = o_t.T.astype(o_ref.dtype)


def _fox_prompt(kaug, qt, qaug, vtaug):
    b, nb, t, kdim = kaug.shape
    tq = min(PROMPT_TQ, t)
    tk = min(FOX_TK, tq)
    return pl.pallas_call(
        functools.partial(_fox_prompt_kernel, tq=tq, tk=tk),
        grid=(b, nb, t // tq),
        in_specs=[
            pl.BlockSpec((1, 1, t, kdim), lambda bi, p, i: (bi, p, 0, 0)),
            pl.BlockSpec((1, 128, tq), lambda bi, p, i: (bi, p, i)),
            pl.BlockSpec((1, HEADS_PER_BLOCK, AUG_ROWS, tq), lambda bi, p, i: (bi, p, 0, i)),
            pl.BlockSpec((1, HEADS_PER_BLOCK, V_ROWS, t), lambda bi, p, i: (bi, p, 0, 0)),
        ],
        out_specs=pl.BlockSpec((1, tq, 128), lambda bi, p, i: (bi, i, p)),
        out_shape=jax.ShapeDtypeStruct((b, t, nb * 128), BF16),
        scratch_shapes=[pltpu.VMEM((HEADS_PER_BLOCK, kdim, tq), BF16)],
        compiler_params=_cparams("parallel", "parallel", "arbitrary"),
        name="fox_prompt",
    )(kaug, qt, qaug, vtaug)


def _sb_prompt_kernel(k_ref, qt_ref, vt_ref, ut_ref, o_ref, q_ref, *, tq, tk, chains):
    i = pl.program_id(2)
    for c, (bb, hh) in enumerate(chains):
        qt2 = qt_ref[bb]
        q_ref[c] = jnp.where(_head_sublane_mask(qt2.shape, hh), qt2, jnp.zeros_like(qt2))
    ut = ut_ref[...]
    cb = ut.shape[0]
    nsub = tk // cb
    krow = lax.broadcasted_iota(jnp.int32, (tk, tq), 0)
    qcol = lax.broadcasted_iota(jnp.int32, (tk, tq), 1)
    sub = tq // tk

    def step(j, carry, diag):
        off = pl.multiple_of(j * tk, tk)
        new = []
        for c, (bb, hh) in enumerate(chains):
            rest, acc = carry[c]
            z = _dot(k_ref[bb, pl.ds(off, tk), :], q_ref[c])
            pr = jnp.maximum(z, jnp.log2(1.0 + jnp.exp2(jnp.minimum(z, EXP2_CLAMP))))
            if diag is not None:
                valid = krow + diag * tk < qcol
                pr = jnp.where(valid, pr, 0.0)
            blocks = [None] * nsub
            for blk in reversed(range(nsub)):
                sl = slice(blk * cb, (blk + 1) * cb)
                upto = _dot(ut, pr[sl].astype(BF16))
                blocks[blk] = jnp.exp2(z[sl] + upto + rest)
                rest = rest + upto[0:1]
            a = blocks[0] if nsub == 1 else jnp.concatenate(blocks, axis=0)
            if diag is not None:
                a = jnp.where(valid, a, 0.0)
            vt = vt_ref[bb, HEAD_DIM * hh:HEAD_DIM * (hh + 1), pl.ds(off, tk)]
            new.append((rest, acc + _dot(vt, a.astype(BF16))))
        return tuple(new)

    carry = tuple((jnp.zeros((1, tq), F32), jnp.zeros((HEAD_DIM, tq), F32)) for _ in chains)
    for d in reversed(range(sub)):
        carry = step(i * sub + d, carry, d)
    final = lax.fori_loop(0, i * sub, lambda n, c: step(i * sub - 1 - n, c, None), carry)
    for bb in range(o_ref.shape[0]):
        rows = [jnp.zeros((HEAD_DIM, tq), F32)] * HEADS_PER_BLOCK
        for c, (cb_, hh) in enumerate(chains):
            if cb_ == bb:
                rows[hh] = final[c][1]
        o_ref[bb] = jnp.concatenate(rows, axis=0).T.astype(o_ref.dtype)


def _sb_prompt_call(krows, qt, vt, *, first_pair, n_pairs, batch_block, chains):
    b, t, _ = krows.shape
    tq = min(PROMPT_TQ, t)
    tk = min(SB_TK, tq)
    cb = min(CUM_BLOCK, tk)
    ut = -(jnp.arange(cb)[None, :] >= jnp.arange(cb)[:, None]).astype(BF16)
    nbb = batch_block
    return pl.pallas_call(
        functools.partial(_sb_prompt_kernel, tq=tq, tk=tk, chains=chains),
        grid=(b // nbb, n_pairs, t // tq),
        in_specs=[
            pl.BlockSpec((nbb, t, 128), lambda bi, p, i: (bi, 0, first_pair + p)),
            pl.BlockSpec((nbb, 128, tq), lambda bi, p, i: (bi, first_pair + p, i)),
            pl.BlockSpec((nbb, 128, t), lambda bi, p, i: (bi, first_pair + p, 0)),
            pl.BlockSpec((cb, cb), lambda bi, p, i: (0, 0)),
        ],
        out_specs=pl.BlockSpec((nbb, tq, 128), lambda bi, p, i: (bi, i, p)),
        out_shape=jax.ShapeDtypeStruct((b, t, n_pairs * 128), BF16),
        scratch_shapes=[pltpu.VMEM((len(chains), 128, tq), BF16)],
        compiler_params=_cparams("parallel", "parallel", "arbitrary"),
        name="sb_prompt",
    )(krows, qt, vt, ut)


def _sb_prompt(krows, qt, vt):
    b = krows.shape[0]
    full, left = divmod(H_SB, HEADS_PER_BLOCK)
    outs = [_sb_prompt_call(krows, qt, vt, first_pair=0, n_pairs=full, batch_block=1,
                            chains=tuple((0, hh) for hh in range(HEADS_PER_BLOCK)))]
    if left:
        nbb = 2 if b % 2 == 0 else 1
        outs.append(_sb_prompt_call(krows, qt, vt, first_pair=full, n_pairs=1, batch_block=nbb,
                                    chains=tuple((bb, 0) for bb in range(nbb))))
    return outs


def _log_gamma(head):
    pw = jnp.zeros(head.shape, F32)
    for h in range(HEADS_PER_BLOCK * (GROUP_W // 128)):
        pw = jnp.where(head == h, 2.0 ** (-5.0 - h), pw)
    return jnp.log(1.0 - pw)


def _ret_prompt_kernel(q_ref, k_ref, v_ref, o_ref, s_ref, state_ref, dec_ref, qkd_ref, *, ch):
    p = pl.program_id(1)
    c = pl.program_id(2)

    @pl.when(c == 0)
    def _():
        state_ref[...] = jnp.zeros_like(state_ref)
        ii = lax.broadcasted_iota(jnp.int32, (ch, ch), 0)
        jj = lax.broadcasted_iota(jnp.int32, (ch, ch), 1)
        diff = ii - jj
        pos = lax.broadcasted_iota(jnp.int32, (ch, 1), 0).astype(F32)
        for hh in range(HEADS_PER_BLOCK):
            lg = _log_gamma(jnp.full((1, 1), HEADS_PER_BLOCK * p + hh, jnp.int32))
            dec_ref[hh] = jnp.where(diff >= 0, jnp.exp(jnp.maximum(diff, 0).astype(F32) * lg), 0.0)
            qkd_ref[hh, :, 0:1] = jnp.exp((pos + 1.0) * lg)
            qkd_ref[hh, :, 1:2] = jnp.exp((ch - 1.0 - pos) * lg)
            qkd_ref[hh, :, 2:3] = jnp.broadcast_to(jnp.exp(ch * lg), (ch, 1))

    q2 = q_ref[0]
    k2 = k_ref[0]
    v2 = v_ref[0]
    outs = []
    for hh in range(HEADS_PER_BLOCK):
        qh = jnp.where(_head_lane_mask(q2.shape, hh), q2, jnp.zeros_like(q2))
        scores = _dot_nt(qh, k2) * dec_ref[hh]
        inner = _dot(scores.astype(BF16), v2)
        state = state_ref[hh]
        q_dec = (qh.astype(F32) * qkd_ref[hh, :, 0:1]).astype(BF16)
        cross = _dot(q_dec, state.astype(BF16))
        outs.append(inner + cross)
        k_dec = (k2.astype(F32) * qkd_ref[hh, :, 1:2]).astype(BF16)
        state_ref[hh] = qkd_ref[hh, 0:1, 2:3] * state + _dot_tn(k_dec, v2)
    o_ref[0] = jnp.where(_head_lane_mask(outs[0].shape, 0), outs[0], outs[1])

    @pl.when(c == pl.num_programs(2) - 1)
    def _():
        s_ref[0, 0] = state_ref[...]


def _ret_prompt(q, k, v):
    b, t, w = q.shape
    ch = min(256, t)
    nb = w // 128
    tile = pl.BlockSpec((1, ch, 128), lambda bi, p, c: (bi, c, p))
    return pl.pallas_call(
        functools.partial(_ret_prompt_kernel, ch=ch),
        grid=(b, nb, t // ch),
        in_specs=[tile, tile, tile],
        out_specs=[tile, pl.BlockSpec((1, 1, HEADS_PER_BLOCK, 128, 128), lambda bi, p, c: (bi, p, 0, 0, 0))],
        out_shape=[jax.ShapeDtypeStruct((b, t, w), F32),
                   jax.ShapeDtypeStruct((b, nb, HEADS_PER_BLOCK, 128, 128), F32)],
        scratch_shapes=[pltpu.VMEM((HEADS_PER_BLOCK, 128, 128), F32),
                        pltpu.VMEM((HEADS_PER_BLOCK, ch, ch), F32),
                        pltpu.VMEM((HEADS_PER_BLOCK, ch, 3), F32)],
        compiler_params=_cparams("parallel", "parallel", "arbitrary"),
        name="ret_prompt",
    )(q, k, v)


def _ret_sample_kernel(q_ref, k_ref, v_ref, s_ref, o_ref, snew_ref, *, n_new):
    q = q_ref[0]
    k = k_ref[0]
    v = v_ref[0]
    ii = lax.broadcasted_iota(jnp.int32, (n_new, n_new), 0)
    jj = lax.broadcasted_iota(jnp.int32, (n_new, n_new), 1)
    diff = ii - jj
    pos = lax.broadcasted_iota(jnp.int32, (n_new, 1), 0).astype(F32)
    o_ref[0] = jnp.zeros(o_ref.shape[1:], F32)
    for h in range(H_RET):
        lg = _log_gamma(jnp.full((1, 1), h, jnp.int32))
        sl = slice(h * HEAD_DIM, (h + 1) * HEAD_DIM)
        qh = q[:, sl]
        kh = k[:, sl]
        vh = v[:, sl].astype(BF16)
        decay = jnp.where(diff >= 0, jnp.exp(jnp.maximum(diff, 0).astype(F32) * lg), 0.0)
        scores = _dot_nt(qh.astype(BF16), kh.astype(BF16)) * decay
        inner = _dot(scores.astype(BF16), vh)
        state = s_ref[0, h]
        cross = _dot((qh * jnp.exp((pos + 1.0) * lg)).astype(BF16), state.astype(BF16))
        o_ref[0, :, sl] = inner + cross
        k_dec = (kh * jnp.exp((n_new - 1.0 - pos) * lg)).astype(BF16)
        snew_ref[0, h] = jnp.exp(n_new * lg) * state + _dot_tn(k_dec, vh)


def _ret_sample(q, k, v, state):
    nb, n_new, w = q.shape
    tok = pl.BlockSpec((1, n_new, w), lambda i: (i, 0, 0))
    st = pl.BlockSpec((1, H_RET, HEAD_DIM, HEAD_DIM), lambda i: (i, 0, 0, 0))
    return pl.pallas_call(
        functools.partial(_ret_sample_kernel, n_new=n_new),
        grid=(nb,),
        in_specs=[tok, tok, tok, st],
        out_specs=[tok, st],
        out_shape=[jax.ShapeDtypeStruct((nb, n_new, w), F32),
                   jax.ShapeDtypeStruct(state.shape, F32)],
        compiler_params=_cparams("parallel"),
        name="ret_sample",
    )(q, k, v, state)


ROWS_PER_TOKEN = 8


def _row_token(shape):
    return lax.broadcasted_iota(jnp.int32, shape, 0) // ROWS_PER_TOKEN


def _collapse_heads(acc, n_new, width):
    r = lax.broadcasted_iota(jnp.int32, acc.shape, 0) % ROWS_PER_TOKEN
    lane_head = lax.broadcasted_iota(jnp.int32, acc.shape, 1) // HEAD_DIM
    kept = jnp.where(r == lane_head, acc, 0.0)
    return jnp.sum(kept.reshape(n_new, ROWS_PER_TOKEN, width), axis=1)


def _fox_decode_kernel(pt_ref, qbd_ref, knew_ref, vnew_ref, lfcol_ref, lfmat_ref, lfc_ref, u_ref, *rest,
                       n_pp, n_new, n_pages):
    k_refs = rest[:n_pp]
    v_refs = rest[n_pp:2 * n_pp]
    o_ref = rest[2 * n_pp]
    m_ref, l_ref, acc_ref, suf_ref, lf_ref = rest[2 * n_pp + 1:]
    b = pl.program_id(0)
    g = pl.program_id(1)
    nrow = n_new * ROWS_PER_TOKEN
    qbd = qbd_ref[0]
    u = u_ref[...]

    lfcol = lfcol_ref[0]
    cnew_col = jnp.concatenate(
        [sum(lfcol[j * ROWS_PER_TOKEN:(j + 1) * ROWS_PER_TOKEN] for j in range(t + 1)) for t in range(n_new)], axis=0)

    def attend(s, pv):
        m_new = jnp.maximum(m_ref[...], jnp.max(s, axis=1, keepdims=True))
        alpha = jnp.exp(m_ref[...] - m_new)
        p = jnp.exp(s - m_new)
        l_ref[...] = alpha * l_ref[...] + jnp.sum(p, axis=1, keepdims=True)
        acc_ref[...] = alpha * acc_ref[...] + pv(p.astype(BF16))
        m_ref[...] = m_new

    @pl.when(g == 0)
    def _():
        m_ref[...] = jnp.full_like(m_ref, NEG_INF)
        l_ref[...] = jnp.zeros_like(l_ref)
        acc_ref[...] = jnp.zeros_like(acc_ref)
        suf_ref[...] = jnp.zeros_like(suf_ref)
        lf_ref[...] = jnp.zeros_like(lf_ref)
        lane = lax.broadcasted_iota(jnp.int32, (nrow, 128), 1)
        lfmat = lfmat_ref[0]
        cnew_mat = jnp.zeros_like(lfmat)
        for j in range(n_new):
            cnew_mat = cnew_mat + jnp.where(lane >= j, lfmat[:, j:j + 1], 0.0)
        s = _dot(qbd, knew_ref[0]) + (cnew_col - cnew_mat)
        s = jnp.where(lane <= _row_token((nrow, 128)), s, NEG_INF)
        attend(s, lambda p: _dot(p, vnew_ref[0]))

    for ii in range(n_pp):
        page = pt_ref[b, n_pages - 1 - (g * n_pp + ii)]
        for h in range(H_FOX):
            lf_ref[pl.ds(ii * ROWS_PER_TOKEN + h, 1), :] = lfc_ref[0, h, pl.ds(page, 1), :]
    lf_all = lf_ref[...]
    within = _dot_acc3(lf_all, u)
    totals = jnp.sum(lf_all, axis=1, keepdims=True)
    later_pages = suf_ref[...]
    scores = []
    for ii in range(n_pp):
        sl = slice(ii * ROWS_PER_TOKEN, (ii + 1) * ROWS_PER_TOKEN)
        bias = jnp.concatenate([within[sl] + later_pages] * n_new, axis=0) + cnew_col
        kp = k_refs[ii][0, 0].reshape(H_FOX * HEAD_DIM, 128).astype(BF16)
        scores.append(_dot(qbd, kp) + bias)
        later_pages = later_pages + totals[sl]
    suf_ref[...] = later_pages

    def pv(p):
        out = None
        for ii in range(n_pp):
            vp = v_refs[ii][0, 0].reshape(H_FOX * HEAD_DIM, 128).astype(BF16)
            part = _dot_nt(p[:, ii * 128:(ii + 1) * 128], vp)
            out = part if out is None else out + part
        return out

    attend(jnp.concatenate(scores, axis=1), pv)

    @pl.when(g == pl.num_programs(1) - 1)
    def _():
        o_ref[0] = _collapse_heads(acc_ref[...] / l_ref[...], n_new, GROUP_W)


def _pages_per_step(n_pages):
    return min(16, n_pages)


def _fox_decode(layer, page_table, qbd, knew_t, vnew, lfcol, lfmat, cache_k, cache_v, cache_lf):
    nb, nrow, w = qbd.shape
    n_new = nrow // ROWS_PER_TOKEN
    n_pages = page_table.shape[1]
    n_pp = _pages_per_step(n_pages)
    u = (jnp.arange(128)[:, None] > jnp.arange(128)[None, :]).astype(BF16)
    per_seq = lambda a: pl.BlockSpec((1,) + a.shape[1:], lambda b, g, pt: (b,) + (0,) * (a.ndim - 1))

    def page_spec(ii):
        return pl.BlockSpec((1, 1) + cache_k.shape[2:],
                            lambda b, g, pt: (layer, pt[b, n_pages - 1 - (g * n_pp + ii)], 0, 0, 0))

    grid_spec = pltpu.PrefetchScalarGridSpec(
        num_scalar_prefetch=1,
        grid=(nb, n_pages // n_pp),
        in_specs=[per_seq(qbd), per_seq(knew_t), per_seq(vnew), per_seq(lfcol), per_seq(lfmat),
                  pl.BlockSpec((1,) + cache_lf.shape[1:], lambda b, g, pt: (layer, 0, 0, 0)),
                  pl.BlockSpec(u.shape, lambda b, g, pt: (0, 0))]
        + [page_spec(ii) for ii in range(n_pp)] * 2,
        out_specs=pl.BlockSpec((1, n_new, w), lambda b, g, pt: (b, 0, 0)),
        scratch_shapes=[pltpu.VMEM((nrow, 1), F32), pltpu.VMEM((nrow, 1), F32), pltpu.VMEM((nrow, w), F32),
                        pltpu.VMEM((ROWS_PER_TOKEN, 1), F32), pltpu.VMEM((n_pp * ROWS_PER_TOKEN, 128), F32)],
    )
    return pl.pallas_call(
        functools.partial(_fox_decode_kernel, n_pp=n_pp, n_new=n_new, n_pages=n_pages),
        grid_spec=grid_spec,
        out_shape=jax.ShapeDtypeStruct((nb, n_new, w), F32),
        compiler_params=_cparams("parallel", "arbitrary"),
        name="fox_decode",
    )(page_table, qbd, knew_t, vnew, lfcol, lfmat, cache_lf, u, *([cache_k] * n_pp), *([cache_v] * n_pp))


def _sb_decode_kernel(pt_ref, qbd_ref, knew_ref, vnew_ref, u_ref, *rest, n_pp, n_new):
    k_refs = rest[:n_pp]
    v_refs = rest[n_pp:2 * n_pp]
    o_ref = rest[2 * n_pp]
    rest_ref, acc_ref = rest[2 * n_pp + 1:]
    g = pl.program_id(1)
    nrow = n_new * ROWS_PER_TOKEN
    width = H_SB * HEAD_DIM
    qbd = qbd_ref[0]
    u = u_ref[...]

    def weights(z_blocks, valid):
        n = len(z_blocks)
        z = jnp.concatenate(z_blocks, axis=0)
        pr = jnp.maximum(z, jnp.log2(1.0 + jnp.exp2(jnp.minimum(z, EXP2_CLAMP))))
        if valid is not None:
            pr = jnp.where(valid, pr, 0.0)
        upto = _dot(pr.astype(BF16), u)
        run = rest_ref[...]
        out = []
        for ii in range(n):
            sl = slice(ii * nrow, (ii + 1) * nrow)
            a = jnp.exp2(z[sl] + upto[sl] + run)
            if valid is not None:
                a = jnp.where(valid, a, 0.0)
            out.append(a.astype(BF16))
            run = run + upto[sl, 0:1]
        rest_ref[...] = run
        return out

    @pl.when(g == 0)
    def _():
        rest_ref[...] = jnp.zeros_like(rest_ref)
        acc_ref[...] = jnp.zeros_like(acc_ref)
        lane = lax.broadcasted_iota(jnp.int32, (nrow, 128), 1)
        (a,) = weights([_dot(qbd, knew_ref[0])], lane < _row_token((nrow, 128)))
        acc_ref[...] = _dot(a, vnew_ref[0])

    z_blocks = [_dot(qbd, k_refs[ii][0, 0].reshape(width, 128).astype(BF16)) for ii in range(n_pp)]
    a_blocks = weights(z_blocks, None)
    out = acc_ref[...]
    for ii in range(n_pp):
        out = out + _dot_nt(a_blocks[ii], v_refs[ii][0, 0].reshape(width, 128).astype(BF16))
    acc_ref[...] = out

    @pl.when(g == pl.num_programs(1) - 1)
    def _():
        o_ref[0] = _collapse_heads(acc_ref[...], n_new, width)


def _sb_decode(layer, page_table, qbd, knew_t, vnew, cache_k, cache_v):
    nb, nrow, width = qbd.shape
    n_new = nrow // ROWS_PER_TOKEN
    n_pages = page_table.shape[1]
    n_pp = _pages_per_step(n_pages)
    u = -(jnp.arange(128)[:, None] >= jnp.arange(128)[None, :]).astype(BF16)
    per_seq = lambda a: pl.BlockSpec((1,) + a.shape[1:], lambda b, g, pt: (b,) + (0,) * (a.ndim - 1))

    def page_spec(ii):
        return pl.BlockSpec((1, 1) + cache_k.shape[2:],
                            lambda b, g, pt: (layer, pt[b, n_pages - 1 - (g * n_pp + ii)], 0, 0, 0))

    grid_spec = pltpu.PrefetchScalarGridSpec(
        num_scalar_prefetch=1,
        grid=(nb, n_pages // n_pp),
        in_specs=[per_seq(qbd), per_seq(knew_t), per_seq(vnew), pl.BlockSpec(u.shape, lambda b, g, pt: (0, 0))]
        + [page_spec(ii) for ii in range(n_pp)] * 2,
        out_specs=pl.BlockSpec((1, n_new, width), lambda b, g, pt: (b, 0, 0)),
        scratch_shapes=[pltpu.VMEM((nrow, 1), F32), pltpu.VMEM((nrow, width), F32)],
    )
    return pl.pallas_call(
        functools.partial(_sb_decode_kernel, n_pp=n_pp, n_new=n_new),
        grid_spec=grid_spec,
        out_shape=jax.ShapeDtypeStruct((nb, n_new, width), F32),
        compiler_params=_cparams("parallel", "arbitrary"),
        name="sb_decode",
    )(page_table, qbd, knew_t, vnew, u, *([cache_k] * n_pp), *([cache_v] * n_pp))


def _pad_cols(a, width=GROUP_W):
    return jnp.pad(a, ((0, 0), (0, width - a.shape[1])))


def _split_mix_weight(w):
    wf, wr, ws = H_FOX * HEAD_DIM, H_RET * HEAD_DIM, H_SB * HEAD_DIM
    sizes = (wf, wf, wf, H_FOX, wr, wr, wr, wr, ws, ws, ws)
    offs = np.cumsum((0,) + sizes)
    return [w[:, offs[i]:offs[i + 1]] for i in range(len(sizes))]


def _rope_swap_perm(n_heads):
    idx = np.arange(n_heads * HEAD_DIM)
    return (idx // HEAD_DIM) * HEAD_DIM + (idx % HEAD_DIM + HEAD_DIM // 2) % HEAD_DIM


def _rope_tables(pos, n_heads):
    half = HEAD_DIM // 2
    inv_freq = ROPE_BASE ** (-jnp.arange(half, dtype=F32) / half)
    ang = pos.astype(F32)[:, None] * inv_freq[None, :]
    cos, sin = jnp.cos(ang), jnp.sin(ang)
    cos_t = jnp.tile(jnp.concatenate([cos, cos], axis=1), (1, n_heads))
    sin_t = jnp.tile(jnp.concatenate([-sin, sin], axis=1), (1, n_heads))
    return _pad_cols(cos_t), _pad_cols(sin_t)


def _block_diag_queries(q, n_heads):
    nb, n_new, w = q.shape
    slot = jnp.arange(ROWS_PER_TOKEN)[:, None]
    lane_head = (jnp.arange(w) // HEAD_DIM)[None, :]
    mask = (slot == lane_head) & (slot < n_heads)
    qb = jnp.where(mask[None, None], q[:, :, None, :], 0.0)
    return qb.reshape(nb, n_new * ROWS_PER_TOKEN, w).astype(BF16)


def _new_keys_t(k, width):
    nb, n_new, _ = k.shape
    kt = jnp.swapaxes(k[:, :, :width], 1, 2)
    return jnp.pad(kt, ((0, 0), (0, 0), (0, 128 - n_new))).astype(BF16)


def _new_values(v, width):
    nb, n_new, _ = v.shape
    return jnp.pad(v[:, :, :width], ((0, 0), (0, 128 - n_new), (0, 0))).astype(BF16)


def kernel(x_prompt, x_sample, cache_fox_k, cache_fox_v, cache_fox_logf, cache_sb_k, cache_sb_v, state_ret,
           page_table, norm_g, w_ffn1_in, w_ffn1_out, w_mix_in, b_forget, ret_gn_g, w_mix_out, w_ffn2_in,
           w_ffn2_out):
    depth = norm_g.shape[0]
    bsz, seq, d = x_prompt.shape
    nb, n_new, _ = x_sample.shape
    n_pages, page_size = page_table.shape[1], cache_fox_k.shape[2]
    past = n_pages * page_size
    wf, wr, ws = H_FOX * HEAD_DIM, H_RET * HEAD_DIM, H_SB * HEAD_DIM
    w = GROUP_W

    ck_fox = jnp.transpose(cache_fox_k, (0, 1, 3, 4, 2))
    cv_fox = jnp.transpose(cache_fox_v, (0, 1, 3, 4, 2))
    ck_sb = jnp.transpose(cache_sb_k, (0, 1, 3, 4, 2))
    cv_sb = jnp.transpose(cache_sb_v, (0, 1, 3, 4, 2))
    clf = jnp.transpose(cache_fox_logf, (0, 3, 1, 2))

    cos_p, sin_p = _rope_tables(jnp.arange(seq, dtype=jnp.int32), H_RET)
    cos_s, sin_s = _rope_tables(past + jnp.arange(n_new, dtype=jnp.int32), H_RET)
    cos_s, sin_s = jnp.tile(cos_s, (nb, 1)), jnp.tile(sin_s, (nb, 1))
    perm = _rope_swap_perm(H_RET)
    avg = jnp.asarray(np.kron(np.eye(w // HEAD_DIM), np.full((HEAD_DIM, HEAD_DIM), 1.0 / HEAD_DIM)), BF16)

    yp = x_prompt.reshape(bsz * seq, d)
    ys = x_sample.reshape(nb * n_new, d)
    st_p, st_s = [], []
    for l in range(depth):
        g = [norm_g[l, i][None, :] for i in range(norm_g.shape[1])]
        w1_in, w1_out = w_ffn1_in[l].astype(BF16), w_ffn1_out[l].astype(BF16)
        w2_in, w2_out = w_ffn2_in[l].astype(BF16), w_ffn2_out[l].astype(BF16)
        qa, ka, va, fa, qr, kr, vr, gr, qs, ks, vs = _split_mix_weight(w_mix_in[l])
        ret_cols = [_pad_cols(qr), _pad_cols(qr[:, perm]), _pad_cols(kr), _pad_cols(kr[:, perm]),
                    _pad_cols(vr), _pad_cols(gr)]
        kv_cols = [ka, va, _pad_cols(ks), _pad_cols(vs)]
        wrow = jnp.concatenate([ka, _pad_cols(ks)] + ret_cols, axis=1).astype(BF16)
        wt = jnp.concatenate(kv_cols + [qa, _pad_cols(qs), _pad_cols(fa, 16)], axis=1).T.astype(BF16)
        w_samp = jnp.concatenate([qa, _pad_cols(qs)] + ret_cols + kv_cols + [_pad_cols(fa, 128)], axis=1).astype(BF16)
        bf_col = jnp.pad(b_forget[l], (0, 16 - H_FOX))[:, None]
        bf_row = jnp.pad(b_forget[l], (0, 128 - H_FOX))[None, :]
        gn = _pad_cols(ret_gn_g[l][None, :])
        wo = w_mix_out[l]
        wo1 = wo[:wf].astype(BF16)
        wo2 = jnp.pad(wo[wf:wf + wr], ((0, w - wr), (0, 0))).astype(BF16)
        wo3 = jnp.pad(wo[wf + wr:], ((0, w - ws), (0, 0))).astype(BF16)

        hp = _ffn_block(yp, g[0], w1_in, w1_out, g[1])
        (ksr, qr_p, kr_p, vr_p, gr_p, kaug, kat, vat, kst, vst, vstb, qat, qst, lft, qaug, vtaug) = _proj_prompt(
            hp.reshape(bsz, seq, d), g[2], wrow, wt, bf_col, cos_p, sin_p)
        o_fox = _fox_prompt(kaug, qat, qaug, vtaug)
        o_ret, s_raw = _ret_prompt(qr_p, kr_p, vr_p)
        o_sb_a, o_sb_b = _sb_prompt(ksr, qst, vstb)
        hp = _merge_out(hp, o_fox.reshape(bsz * seq, w), o_ret.reshape(bsz * seq, w), gr_p.reshape(bsz * seq, w),
                        o_sb_a.reshape(bsz * seq, -1), o_sb_b.reshape(bsz * seq, -1), gn, avg, wo1, wo2, wo3, g[3])
        yp = _ffn_block(hp, g[4], w2_in, w2_out, g[5])
        s_fin = jnp.stack([s_raw[:, h // 2, h % 2, (h % 2) * HEAD_DIM:(h % 2 + 1) * HEAD_DIM,
                                 (h % 2) * HEAD_DIM:(h % 2 + 1) * HEAD_DIM] for h in range(H_RET)], axis=1)
        to_heads = lambda a, nh: jnp.transpose(a.reshape(bsz, nh, HEAD_DIM, seq), (0, 3, 1, 2))
        st_p.append((to_heads(kat, H_FOX), to_heads(vat, H_FOX), jnp.transpose(lft[:, :H_FOX], (0, 2, 1)),
                     to_heads(kst, H_SB), to_heads(vst, H_SB), s_fin))

        hs = _ffn_block(ys, g[0], w1_in, w1_out, g[1])
        (qa_s, qs_s, qr_s, kr_s, vr_s, gr_s, ka_s, va_s, ks_s, vs_s, lf_s) = _proj_sample(
            hs, g[2], w_samp, bf_row, cos_s, sin_s)
        seqs = lambda a: a.reshape(nb, n_new, a.shape[-1])
        lf_new = seqs(lf_s)[:, :, :ROWS_PER_TOKEN]
        lfcol = lf_new.reshape(nb, n_new * ROWS_PER_TOKEN, 1)
        lfmat = jnp.broadcast_to(jnp.transpose(lf_new, (0, 2, 1))[:, None], (nb, n_new, ROWS_PER_TOKEN, n_new))
        lfmat = jnp.pad(lfmat.reshape(nb, n_new * ROWS_PER_TOKEN, n_new), ((0, 0), (0, 0), (0, 128 - n_new)))
        o_fox_s = _fox_decode(l, page_table, _block_diag_queries(seqs(qa_s), H_FOX), _new_keys_t(seqs(ka_s), wf),
                              _new_values(seqs(va_s), wf), lfcol, lfmat, ck_fox, cv_fox, clf)
        o_sb_s = _sb_decode(l, page_table, _block_diag_queries(seqs(qs_s)[:, :, :ws] * LOG2E, H_SB),
                            _new_keys_t(seqs(ks_s), ws), _new_values(seqs(vs_s), ws), ck_sb, cv_sb)
        o_ret_s, s_new = _ret_sample(seqs(qr_s), seqs(kr_s), seqs(vr_s), state_ret[l])
        o_sb_rows = _pad_cols(o_sb_s.reshape(nb * n_new, ws)).astype(BF16)
        hs = _merge_out(hs, o_fox_s.reshape(nb * n_new, w).astype(BF16), o_ret_s.reshape(nb * n_new, w), gr_s,
                        o_sb_rows[:, :o_sb_a.shape[-1]], o_sb_rows[:, o_sb_a.shape[-1]:], gn, avg, wo1, wo2, wo3, g[3])
        ys = _ffn_block(hs, g[4], w2_in, w2_out, g[5])
        heads = lambda a, nh: a[:, :nh * HEAD_DIM].reshape(nb, n_new, nh, HEAD_DIM)
        st_s.append((heads(ka_s, H_FOX), heads(va_s, H_FOX), seqs(lf_s)[:, :, :H_FOX],
                     heads(ks_s, H_SB), heads(vs_s, H_SB), s_new))

    stk = lambda sts, i: jnp.stack([s[i] for s in sts], axis=0)
    return (yp.reshape(bsz, seq, d), ys.reshape(nb, n_new, d),
            stk(st_p, 0), stk(st_p, 1), stk(st_p, 2), stk(st_p, 3), stk(st_p, 4), stk(st_p, 5),
            stk(st_s, 0), stk(st_s, 1), stk(st_s, 2), stk(st_s, 3), stk(st_s, 4), stk(st_s, 5))
```

```python
import functools
import math

import numpy as np
import jax
import jax.numpy as jnp
from jax import lax
from jax.experimental import pallas as pl
from jax.experimental.pallas import tpu as pltpu

HEAD_DIM = 64
H_FOX = 6
H_RET = 5
H_SB = 5
GROUP_W = 384
HEADS_PER_BLOCK = 2
EPS = 1e-6
ROPE_BASE = 10000.0
Q_SCALE = HEAD_DIM ** -0.5
LOG2E = math.log2(math.e)
AUG_ROWS = 16
V_ROWS = 80
CUM_BLOCK = 256
EXP2_CLAMP = 64.0
FF_CHUNK = 256
RET_CHUNK = 512
PROMPT_TQ = 1024
FOX_TK = 1024
SB_TK = 1024
VMEM_LIMIT_BYTES = 48 * 1024 * 1024
F32 = jnp.float32
BF16 = jnp.bfloat16
NEG_INF = float("-inf")


def _cparams(*sem):
    return pltpu.CompilerParams(dimension_semantics=sem, vmem_limit_bytes=VMEM_LIMIT_BYTES)


def _dot(a, b):
    return jnp.dot(a, b, preferred_element_type=F32)


def _dot_nt(a, b):
    return lax.dot_general(a, b, (((1,), (1,)), ((), ())), preferred_element_type=F32)


def _dot_tn(a, b):
    return lax.dot_general(a, b, (((0,), (0,)), ((), ())), preferred_element_type=F32)


def _split3(x):
    hi = x.astype(BF16)
    r = x - hi.astype(F32)
    mid = r.astype(BF16)
    lo = (r - mid.astype(F32)).astype(BF16)
    return hi, mid, lo


def _dot_acc3(x, m):
    hi, mid, lo = _split3(x)
    return _dot(hi, m) + _dot(mid, m) + _dot(lo, m)


def _rms(x, g):
    return x * lax.rsqrt(jnp.mean(x * x, axis=-1, keepdims=True) + EPS) * g


def _softplus_parts(z):
    lp = jnp.log1p(jnp.exp(-jnp.abs(z)))
    return -jnp.maximum(z, 0.0) - lp, jnp.minimum(z, 0.0) - lp


def _ffn_kernel(x_ref, gpre_ref, wg_ref, wu_ref, wo_ref, gpost_ref, o_ref):
    x = x_ref[...]
    xn = _rms(x, gpre_ref[...]).astype(BF16)
    hidden = []
    for c in range(wg_ref.shape[1] // FF_CHUNK):
        sl = slice(c * FF_CHUNK, (c + 1) * FF_CHUNK)
        gate = _dot(xn, wg_ref[:, sl])
        up = _dot(xn, wu_ref[:, sl])
        hidden.append((gate * jax.nn.sigmoid(gate) * up).astype(BF16))
    y = _dot(jnp.concatenate(hidden, axis=1), wo_ref[...])
    o_ref[...] = x + 0.5 * _rms(y, gpost_ref[...])


def _ffn_block(x, g_pre, w_in, w_out, g_post):
    m, d = x.shape
    dff = w_out.shape[0]
    tm = min(512, m)
    once = pl.Buffered(1)
    return pl.pallas_call(
        _ffn_kernel,
        grid=(m // tm,),
        in_specs=[
            pl.BlockSpec((tm, d), lambda i: (i, 0)),
            pl.BlockSpec((1, d), lambda i: (0, 0)),
            pl.BlockSpec((d, dff), lambda i: (0, 0), pipeline_mode=once),
            pl.BlockSpec((d, dff), lambda i: (0, 1), pipeline_mode=once),
            pl.BlockSpec((dff, d), lambda i: (0, 0), pipeline_mode=once),
            pl.BlockSpec((1, d), lambda i: (0, 0)),
        ],
        out_specs=pl.BlockSpec((tm, d), lambda i: (i, 0)),
        out_shape=jax.ShapeDtypeStruct((m, d), F32),
        compiler_params=_cparams("parallel"),
        name="ffn_block",
    )(x, g_pre, w_in, w_in, w_out, g_post)


def _proj_prompt_kernel(x_ref, g_ref, wrow_ref, wt_ref, bf_ref, cos_ref, sin_ref, tri_ref, kplace_ref, qplace_ref,
                        ksr_ref, qr_ref, kr_ref, vr_ref, gr_ref, kaug_ref,
                        kat_ref, vat_ref, kst_ref, vst_ref,
                        vstb_ref, qat_ref, qst_ref,
                        lf_ref, qaug_ref, vtaug_ref, carry_ref):
    w = GROUP_W
    t = pl.program_id(1)

    @pl.when(t == 0)
    def _():
        carry_ref[...] = jnp.zeros_like(carry_ref)

    xn = _rms(x_ref[0], g_ref[...]).astype(BF16)
    pr = _dot(xn, wrow_ref[...])
    cos = cos_ref[...]
    sin = sin_ref[...]
    ksr_ref[0] = pr[:, w:2 * w].astype(BF16)
    qr_ref[0] = (pr[:, 2 * w:3 * w] * cos + pr[:, 3 * w:4 * w] * sin).astype(BF16)
    kr_ref[0] = ((pr[:, 4 * w:5 * w] * cos + pr[:, 5 * w:6 * w] * sin) * Q_SCALE).astype(BF16)
    vr_ref[0] = pr[:, 6 * w:7 * w].astype(BF16)
    gr_ref[0] = pr[:, 7 * w:8 * w]

    pt = _dot_nt(wt_ref[...], xn)
    kat = pt[0:w]
    vat = pt[w:2 * w]
    kst = pt[2 * w:3 * w]
    vst = pt[3 * w:4 * w]
    kat_ref[0] = kat
    vat_ref[0] = vat
    kst_ref[0] = kst[0:H_SB * HEAD_DIM]
    vst_ref[0] = vst[0:H_SB * HEAD_DIM]
    vstb_ref[0] = vst.astype(BF16)
    qat_ref[0] = (pt[4 * w:5 * w] * (Q_SCALE * LOG2E)).astype(BF16)
    qst_ref[0] = (pt[5 * w:6 * w] * (Q_SCALE * LOG2E)).astype(BF16)

    logf = jax.nn.log_sigmoid(pt[6 * w:6 * w + 16] + bf_ref[...])
    lf_ref[0] = logf
    c = _dot_acc3(logf, tri_ref[...]) + carry_ref[...]
    carry_ref[...] = c[:, -1:]

    tm = c.shape[1]
    one_row = jnp.concatenate([jnp.ones((1, tm), BF16), jnp.zeros((15, tm), BF16)], axis=0)
    csplit = jnp.concatenate(list(_split3(c * LOG2E)) + [one_row], axis=0)
    for p in range(w // 128):
        kaug_ref[0, p, :, 0:128] = pr[:, p * 128:(p + 1) * 128].astype(BF16)
        kaug_ref[0, p, :, 128:256] = _dot_tn(csplit, kplace_ref[p]).astype(BF16)
    q_rows = _dot(qplace_ref[...], csplit).astype(BF16)
    for h in range(H_FOX):
        qaug_ref[0, h] = q_rows[h * AUG_ROWS:(h + 1) * AUG_ROWS]
        vtaug_ref[0, h, 0:HEAD_DIM, :] = vat[h * HEAD_DIM:(h + 1) * HEAD_DIM].astype(BF16)
        vtaug_ref[0, h, HEAD_DIM:V_ROWS, :] = one_row


def _fox_placements():
    nb = GROUP_W // 128
    kplace = np.zeros((nb, 64, 128), np.float32)
    qplace = np.zeros((H_FOX * AUG_ROWS, 64), np.float32)
    for h in range(H_FOX):
        p, hh = divmod(h, HEADS_PER_BLOCK)
        for sp in range(3):
            kplace[p, sp * 16 + h, hh * 3 + sp] = -1.0
            kplace[p, 48, HEADS_PER_BLOCK * 3 + sp] = 1.0
            qplace[h * AUG_ROWS + hh * 3 + sp, 48] = 1.0
            qplace[h * AUG_ROWS + HEADS_PER_BLOCK * 3 + sp, sp * 16 + h] = 1.0
    return jnp.asarray(kplace, BF16), jnp.asarray(qplace, BF16)


def _proj_prompt(x, g, wrow, wt, bf, cos, sin):
    b, t, d = x.shape
    tm = min(512, t)
    w = GROUP_W
    nb = w // 128
    tri = (jnp.arange(tm)[:, None] <= jnp.arange(tm)[None, :]).astype(BF16)
    kplace, qplace = _fox_placements()
    row = lambda dt: jax.ShapeDtypeStruct((b, t, w), dt)
    col = lambda n, dt: jax.ShapeDtypeStruct((b, n, t), dt)
    rspec = pl.BlockSpec((1, tm, w), lambda bi, ti: (bi, ti, 0))
    cspec = lambda n: pl.BlockSpec((1, n, tm), lambda bi, ti: (bi, 0, ti))
    const = lambda a: pl.BlockSpec(a.shape, lambda bi, ti: (0,) * a.ndim)
    nsb = H_SB * HEAD_DIM
    return pl.pallas_call(
        _proj_prompt_kernel,
        grid=(b, t // tm),
        in_specs=[
            pl.BlockSpec((1, tm, d), lambda bi, ti: (bi, ti, 0)),
            const(g), const(wrow), const(wt), const(bf),
            pl.BlockSpec((tm, w), lambda bi, ti: (ti, 0)),
            pl.BlockSpec((tm, w), lambda bi, ti: (ti, 0)),
            const(tri), const(kplace), const(qplace),
        ],
        out_specs=[rspec] * 5 + [pl.BlockSpec((1, nb, tm, 256), lambda bi, ti: (bi, 0, ti, 0))]
        + [cspec(w), cspec(w), cspec(nsb), cspec(nsb)] + [cspec(w)] * 3 + [cspec(16)]
        + [pl.BlockSpec((1, H_FOX, AUG_ROWS, tm), lambda bi, ti: (bi, 0, 0, ti)),
           pl.BlockSpec((1, H_FOX, V_ROWS, tm), lambda bi, ti: (bi, 0, 0, ti))],
        out_shape=[row(BF16)] * 4 + [row(F32), jax.ShapeDtypeStruct((b, nb, t, 256), BF16)]
        + [col(w, F32), col(w, F32), col(nsb, F32), col(nsb, F32)] + [col(w, BF16)] * 3 + [col(16, F32)]
        + [jax.ShapeDtypeStruct((b, H_FOX, AUG_ROWS, t), BF16), jax.ShapeDtypeStruct((b, H_FOX, V_ROWS, t), BF16)],
        scratch_shapes=[pltpu.VMEM((16, 1), F32)],
        compiler_params=_cparams("parallel", "arbitrary"),
        name="proj_prompt",
    )(x, g, wrow, wt, bf, cos, sin, tri, kplace, qplace)


def _proj_sample_kernel(x_ref, g_ref, w_ref, bf_ref, cos_ref, sin_ref,
                        qa_ref, qs_ref, qr_ref, kr_ref, vr_ref, gr_ref,
                        ka_ref, va_ref, ks_ref, vs_ref, lf_ref):
    w = GROUP_W
    xn = _rms(x_ref[...], g_ref[...]).astype(BF16)
    pr = _dot(xn, w_ref[...])
    cos = cos_ref[...]
    sin = sin_ref[...]
    qa_ref[...] = pr[:, 0:w] * Q_SCALE
    qs_ref[...] = pr[:, w:2 * w] * Q_SCALE
    qr_ref[...] = pr[:, 2 * w:3 * w] * cos + pr[:, 3 * w:4 * w] * sin
    kr_ref[...] = (pr[:, 4 * w:5 * w] * cos + pr[:, 5 * w:6 * w] * sin) * Q_SCALE
    vr_ref[...] = pr[:, 6 * w:7 * w]
    gr_ref[...] = pr[:, 7 * w:8 * w]
    ka_ref[...] = pr[:, 8 * w:9 * w]
    va_ref[...] = pr[:, 9 * w:10 * w]
    ks_ref[...] = pr[:, 10 * w:11 * w]
    vs_ref[...] = pr[:, 11 * w:12 * w]
    lf_ref[...] = jax.nn.log_sigmoid(pr[:, 12 * w:12 * w + 128] + bf_ref[...])


def _proj_sample(x, g, w_all, bf_row, cos, sin):
    m = x.shape[0]
    w = GROUP_W
    return pl.pallas_call(
        _proj_sample_kernel,
        out_shape=[jax.ShapeDtypeStruct((m, w), F32)] * 10 + [jax.ShapeDtypeStruct((m, 128), F32)],
        compiler_params=pltpu.CompilerParams(vmem_limit_bytes=VMEM_LIMIT_BYTES),
        name="proj_sample",
    )(x, g, w_all, bf_row, cos, sin)


def _merge_kernel(h_ref, of_ref, oret_ref, gr_ref, osba_ref, osbb_ref, gn_ref, avg_ref,
                  w1_ref, w2_ref, w3a_ref, w3b_ref, g_ref, o_ref):
    x = oret_ref[...]
    avg = avg_ref[...]
    hi, mid, lo = _split3(x)
    mu = _dot(hi, avg) + _dot(mid, avg) + _dot(lo, avg)
    dev = x - mu
    var = _dot_acc3(dev * dev, avg)
    gate = gr_ref[...]
    r = dev * lax.rsqrt(var + EPS) * gn_ref[...] * (gate * jax.nn.sigmoid(gate))
    y = (_dot(of_ref[...], w1_ref[...]) + _dot(r.astype(BF16), w2_ref[...])
         + _dot(osba_ref[...], w3a_ref[...]) + _dot(osbb_ref[...], w3b_ref[...]))
    o_ref[...] = h_ref[...] + _rms(y, g_ref[...])


def _merge_out(h, o_fox, o_ret, gr, o_sb_a, o_sb_b, gn, avg, w1, w2, w3, g):
    m, d = h.shape
    w = GROUP_W
    wa = o_sb_a.shape[1]
    w3a, w3b = w3[:wa], w3[wa:]
    tm = min(512, m)
    rows = lambda n: pl.BlockSpec((tm, n), lambda i: (i, 0))
    const = lambda a: pl.BlockSpec(a.shape, lambda i: (0,) * a.ndim)
    return pl.pallas_call(
        _merge_kernel,
        grid=(m // tm,),
        in_specs=[rows(d), rows(w), rows(w), rows(w), rows(wa), rows(w - wa), const(gn), const(avg),
                  const(w1), const(w2), const(w3a), const(w3b), const(g)],
        out_specs=rows(d),
        out_shape=jax.ShapeDtypeStruct((m, d), F32),
        compiler_params=_cparams("parallel"),
        name="merge_out",
    )(h, o_fox, o_ret, gr, o_sb_a, o_sb_b, gn, avg, w1, w2, w3a, w3b, g)


def _head_lane_mask(shape, hh):
    lane = lax.broadcasted_iota(jnp.int32, shape, len(shape) - 1)
    return (lane >= HEAD_DIM * hh) & (lane < HEAD_DIM * (hh + 1))


def _head_sublane_mask(shape, hh):
    r = lax.broadcasted_iota(jnp.int32, shape, 0)
    return (r >= HEAD_DIM * hh) & (r < HEAD_DIM * (hh + 1))


def _fox_prompt_kernel(kaug_ref, qt_ref, qaug_ref, vt_ref, o_ref, qa_ref, *, tq, tk):
    i = pl.program_id(2)
    qt2 = qt_ref[0]
    kdim = kaug_ref.shape[-1]
    zpad = jnp.zeros((kdim - 128 - AUG_ROWS, tq), BF16)
    for hh in range(HEADS_PER_BLOCK):
        qa_ref[hh] = jnp.concatenate(
            [jnp.where(_head_sublane_mask(qt2.shape, hh), qt2, jnp.zeros_like(qt2)), qaug_ref[0, hh], zpad], axis=0)
    krow = lax.broadcasted_iota(jnp.int32, (tk, tq), 0)
    qcol = lax.broadcasted_iota(jnp.int32, (tk, tq), 1)
    sub = tq // tk

    def step(j, carry, diag):
        off = pl.multiple_of(j * tk, tk)
        kt = kaug_ref[0, 0, pl.ds(off, tk), :]
        new = []
        for hh in range(HEADS_PER_BLOCK):
            m, acc = carry[hh]
            s = _dot(kt, qa_ref[hh])
            if diag is not None:
                s = jnp.where(krow + diag * tk <= qcol, s, NEG_INF)
            m_new = jnp.maximum(m, jnp.max(s, axis=0, keepdims=True))
            alpha = jnp.exp2(m - m_new)
            p = jnp.exp2(s - m_new).astype(BF16)
            vt = vt_ref[0, hh, :, pl.ds(off, tk)]
            new.append((m_new, alpha * acc + _dot(vt, p)))
        return tuple(new)

    init = tuple((jnp.full((1, tq), NEG_INF, F32), jnp.zeros((V_ROWS, tq), F32)) for _ in range(HEADS_PER_BLOCK))
    carry = lax.fori_loop(0, i * sub, functools.partial(step, diag=None), init)
    for d in range(sub):
        carry = step(i * sub + d, carry, d)
    o_t = jnp.concatenate([acc[0:HEAD_DIM] / acc[HEAD_DIM:HEAD_DIM + 1] for _, acc in carry], axis=0)
    o_ref[0] = o_t.T.astype(o_ref.dtype)


def _fox_prompt(kaug, qt, qaug, vtaug):
    b, nb, t, kdim = kaug.shape
    tq = min(PROMPT_TQ, t)
    tk = min(FOX_TK, tq)
    return pl.pallas_call(
        functools.partial(_fox_prompt_kernel, tq=tq, tk=tk),
        grid=(b, nb, t // tq),
        in_specs=[
            pl.BlockSpec((1, 1, t, kdim), lambda bi, p, i: (bi, p, 0, 0)),
            pl.BlockSpec((1, 128, tq), lambda bi, p, i: (bi, p, i)),
            pl.BlockSpec((1, HEADS_PER_BLOCK, AUG_ROWS, tq), lambda bi, p, i: (bi, p, 0, i)),
            pl.BlockSpec((1, HEADS_PER_BLOCK, V_ROWS, t), lambda bi, p, i: (bi, p, 0, 0)),
        ],
        out_specs=pl.BlockSpec((1, tq, 128), lambda bi, p, i: (bi, i, p)),
        out_shape=jax.ShapeDtypeStruct((b, t, nb * 128), BF16),
        scratch_shapes=[pltpu.VMEM((HEADS_PER_BLOCK, kdim, tq), BF16)],
        compiler_params=_cparams("parallel", "parallel", "arbitrary"),
        name="fox_prompt",
    )(kaug, qt, qaug, vtaug)


def _sb_prompt_kernel(k_ref, qt_ref, vt_ref, ut_ref, o_ref, q_ref, *, tq, tk, chains):
    i = pl.program_id(2)
    for c, (bb, hh) in enumerate(chains):
        qt2 = qt_ref[bb]
        q_ref[c] = jnp.where(_head_sublane_mask(qt2.shape, hh), qt2, jnp.zeros_like(qt2))
    ut = ut_ref[...]
    cb = ut.shape[0]
    nsub = tk // cb
    krow = lax.broadcasted_iota(jnp.int32, (tk, tq), 0)
    qcol = lax.broadcasted_iota(jnp.int32, (tk, tq), 1)
    sub = tq // tk

    def step(j, carry, diag):
        off = pl.multiple_of(j * tk, tk)
        new = []
        for c, (bb, hh) in enumerate(chains):
            rest, acc = carry[c]
            z = _dot(k_ref[bb, pl.ds(off, tk), :], q_ref[c])
            pr = jnp.maximum(z, jnp.log2(1.0 + jnp.exp2(jnp.minimum(z, EXP2_CLAMP))))
            if diag is not None:
                valid = krow + diag * tk < qcol
                pr = jnp.where(valid, pr, 0.0)
            blocks = [None] * nsub
            for blk in reversed(range(nsub)):
                sl = slice(blk * cb, (blk + 1) * cb)
                upto = _dot(ut, pr[sl].astype(BF16))
                blocks[blk] = jnp.exp2(z[sl] + upto + rest)
                rest = rest + upto[0:1]
            a = blocks[0] if nsub == 1 else jnp.concatenate(blocks, axis=0)
            if diag is not None:
                a = jnp.where(valid, a, 0.0)
            vt = vt_ref[bb, HEAD_DIM * hh:HEAD_DIM * (hh + 1), pl.ds(off, tk)]
            new.append((rest, acc + _dot(vt, a.astype(BF16))))
        return tuple(new)

    carry = tuple((jnp.zeros((1, tq), F32), jnp.zeros((HEAD_DIM, tq), F32)) for _ in chains)
    for d in reversed(range(sub)):
        carry = step(i * sub + d, carry, d)
    final = lax.fori_loop(0, i * sub, lambda n, c: step(i * sub - 1 - n, c, None), carry)
    for bb in range(o_ref.shape[0]):
        rows = [jnp.zeros((HEAD_DIM, tq), F32)] * HEADS_PER_BLOCK
        for c, (cb_, hh) in enumerate(chains):
            if cb_ == bb:
                rows[hh] = final[c][1]
        o_ref[bb] = jnp.concatenate(rows, axis=0).T.astype(o_ref.dtype)


def _sb_prompt_call(krows, qt, vt, *, first_pair, n_pairs, batch_block, chains):
    b, t, _ = krows.shape
    tq = min(PROMPT_TQ, t)
    tk = min(SB_TK, tq)
    cb = min(CUM_BLOCK, tk)
    ut = -(jnp.arange(cb)[None, :] >= jnp.arange(cb)[:, None]).astype(BF16)
    nbb = batch_block
    return pl.pallas_call(
        functools.partial(_sb_prompt_kernel, tq=tq, tk=tk, chains=chains),
        grid=(b // nbb, n_pairs, t // tq),
        in_specs=[
            pl.BlockSpec((nbb, t, 128), lambda bi, p, i: (bi, 0, first_pair + p)),
            pl.BlockSpec((nbb, 128, tq), lambda bi, p, i: (bi, first_pair + p, i)),
            pl.BlockSpec((nbb, 128, t), lambda bi, p, i: (bi, first_pair + p, 0)),
            pl.BlockSpec((cb, cb), lambda bi, p, i: (0, 0)),
        ],
        out_specs=pl.BlockSpec((nbb, tq, 128), lambda bi, p, i: (bi, i, p)),
        out_shape=jax.ShapeDtypeStruct((b, t, n_pairs * 128), BF16),
        scratch_shapes=[pltpu.VMEM((len(chains), 128, tq), BF16)],
        compiler_params=_cparams("parallel", "parallel", "arbitrary"),
        name="sb_prompt",
    )(krows, qt, vt, ut)


def _sb_prompt(krows, qt, vt):
    b = krows.shape[0]
    full, left = divmod(H_SB, HEADS_PER_BLOCK)
    outs = [_sb_prompt_call(krows, qt, vt, first_pair=0, n_pairs=full, batch_block=1,
                            chains=tuple((0, hh) for hh in range(HEADS_PER_BLOCK)))]
    if left:
        nbb = 2 if b % 2 == 0 else 1
        outs.append(_sb_prompt_call(krows, qt, vt, first_pair=full, n_pairs=1, batch_block=nbb,
                                    chains=tuple((bb, 0) for bb in range(nbb))))
    return outs


def _log_gamma(head):
    pw = jnp.zeros(head.shape, F32)
    for h in range(HEADS_PER_BLOCK * (GROUP_W // 128)):
        pw = jnp.where(head == h, 2.0 ** (-5.0 - h), pw)
    return jnp.log(1.0 - pw)


def _ret_prompt_kernel(q_ref, k_ref, v_ref, o_ref, s_ref, state_ref, dec_ref, qkd_ref, *, ch):
    p = pl.program_id(1)
    c = pl.program_id(2)

    @pl.when(c == 0)
    def _():
        state_ref[...] = jnp.zeros_like(state_ref)
        ii = lax.broadcasted_iota(jnp.int32, (ch, ch), 0)
        jj = lax.broadcasted_iota(jnp.int32, (ch, ch), 1)
        diff = ii - jj
        pos = lax.broadcasted_iota(jnp.int32, (ch, 1), 0).astype(F32)
        for hh in range(HEADS_PER_BLOCK):
            lg = _log_gamma(jnp.full((1, 1), HEADS_PER_BLOCK * p + hh, jnp.int32))
            dec_ref[hh] = jnp.where(diff >= 0, jnp.exp(jnp.maximum(diff, 0).astype(F32) * lg), 0.0)
            qkd_ref[hh, :, 0:1] = jnp.exp((pos + 1.0) * lg)
            qkd_ref[hh, :, 1:2] = jnp.exp((ch - 1.0 - pos) * lg)
            qkd_ref[hh, :, 2:3] = jnp.broadcast_to(jnp.exp(ch * lg), (ch, 1))

    q2 = q_ref[0]
    k2 = k_ref[0]
    v2 = v_ref[0]
    outs = []
    for hh in range(HEADS_PER_BLOCK):
        qh = jnp.where(_head_lane_mask(q2.shape, hh), q2, jnp.zeros_like(q2))
        scores = _dot_nt(qh, k2) * dec_ref[hh]
        inner = _dot(scores.astype(BF16), v2)
        state = state_ref[hh]
        q_dec = (qh.astype(F32) * qkd_ref[hh, :, 0:1]).astype(BF16)
        cross = _dot(q_dec, state.astype(BF16))
        outs.append(inner + cross)
        k_dec = (k2.astype(F32) * qkd_ref[hh, :, 1:2]).astype(BF16)
        state_ref[hh] = qkd_ref[hh, 0:1, 2:3] * state + _dot_tn(k_dec, v2)
    o_ref[0] = jnp.where(_head_lane_mask(outs[0].shape, 0), outs[0], outs[1])

    @pl.when(c == pl.num_programs(2) - 1)
    def _():
        s_ref[0, 0] = state_ref[...]


def _ret_prompt(q, k, v):
    b, t, w = q.shape
    ch = min(RET_CHUNK, t)
    nb = w // 128
    tile = pl.BlockSpec((1, ch, 128), lambda bi, p, c: (bi, c, p))
    return pl.pallas_call(
        functools.partial(_ret_prompt_kernel, ch=ch),
        grid=(b, nb, t // ch),
        in_specs=[tile, tile, tile],
        out_specs=[tile, pl.BlockSpec((1, 1, HEADS_PER_BLOCK, 128, 128), lambda bi, p, c: (bi, p, 0, 0, 0))],
        out_shape=[jax.ShapeDtypeStruct((b, t, w), F32),
                   jax.ShapeDtypeStruct((b, nb, HEADS_PER_BLOCK, 128, 128), F32)],
        scratch_shapes=[pltpu.VMEM((HEADS_PER_BLOCK, 128, 128), F32),
                        pltpu.VMEM((HEADS_PER_BLOCK, ch, ch), F32),
                        pltpu.VMEM((HEADS_PER_BLOCK, ch, 3), F32)],
        compiler_params=_cparams("parallel", "parallel", "arbitrary"),
        name="ret_prompt",
    )(q, k, v)


def _ret_sample_kernel(q_ref, k_ref, v_ref, s_ref, o_ref, snew_ref, *, n_new):
    q = q_ref[0]
    k = k_ref[0]
    v = v_ref[0]
    ii = lax.broadcasted_iota(jnp.int32, (n_new, n_new), 0)
    jj = lax.broadcasted_iota(jnp.int32, (n_new, n_new), 1)
    diff = ii - jj
    pos = lax.broadcasted_iota(jnp.int32, (n_new, 1), 0).astype(F32)
    o_ref[0] = jnp.zeros(o_ref.shape[1:], F32)
    for h in range(H_RET):
        lg = _log_gamma(jnp.full((1, 1), h, jnp.int32))
        sl = slice(h * HEAD_DIM, (h + 1) * HEAD_DIM)
        qh = q[:, sl]
        kh = k[:, sl]
        vh = v[:, sl].astype(BF16)
        decay = jnp.where(diff >= 0, jnp.exp(jnp.maximum(diff, 0).astype(F32) * lg), 0.0)
        scores = _dot_nt(qh.astype(BF16), kh.astype(BF16)) * decay
        inner = _dot(scores.astype(BF16), vh)
        state = s_ref[0, h]
        cross = _dot((qh * jnp.exp((pos + 1.0) * lg)).astype(BF16), state.astype(BF16))
        o_ref[0, :, sl] = inner + cross
        k_dec = (kh * jnp.exp((n_new - 1.0 - pos) * lg)).astype(BF16)
        snew_ref[0, h] = jnp.exp(n_new * lg) * state + _dot_tn(k_dec, vh)


def _ret_sample(q, k, v, state):
    nb, n_new, w = q.shape
    tok = pl.BlockSpec((1, n_new, w), lambda i: (i, 0, 0))
    st = pl.BlockSpec((1, H_RET, HEAD_DIM, HEAD_DIM), lambda i: (i, 0, 0, 0))
    return pl.pallas_call(
        functools.partial(_ret_sample_kernel, n_new=n_new),
        grid=(nb,),
        in_specs=[tok, tok, tok, st],
        out_specs=[tok, st],
        out_shape=[jax.ShapeDtypeStruct((nb, n_new, w), F32),
                   jax.ShapeDtypeStruct(state.shape, F32)],
        compiler_params=_cparams("parallel"),
        name="ret_sample",
    )(q, k, v, state)


ROWS_PER_TOKEN = 8


def _row_token(shape):
    return lax.broadcasted_iota(jnp.int32, shape, 0) // ROWS_PER_TOKEN


def _collapse_heads(acc, n_new, width):
    r = lax.broadcasted_iota(jnp.int32, acc.shape, 0) % ROWS_PER_TOKEN
    lane_head = lax.broadcasted_iota(jnp.int32, acc.shape, 1) // HEAD_DIM
    kept = jnp.where(r == lane_head, acc, 0.0)
    return jnp.sum(kept.reshape(n_new, ROWS_PER_TOKEN, width), axis=1)


def _fox_decode_kernel(pt_ref, qbd_ref, knew_ref, vnew_ref, lfcol_ref, lfmat_ref, lfc_ref, u_ref, *rest,
                       n_pp, n_new, n_pages):
    k_refs = rest[:n_pp]
    v_refs = rest[n_pp:2 * n_pp]
    o_ref = rest[2 * n_pp]
    m_ref, l_ref, acc_ref, suf_ref, lf_ref = rest[2 * n_pp + 1:]
    b = pl.program_id(0)
    g = pl.program_id(1)
    nrow = n_new * ROWS_PER_TOKEN
    qbd = qbd_ref[0]
    u = u_ref[...]

    lfcol = lfcol_ref[0]
    cnew_col = jnp.concatenate(
        [sum(lfcol[j * ROWS_PER_TOKEN:(j + 1) * ROWS_PER_TOKEN] for j in range(t + 1)) for t in range(n_new)], axis=0)

    def attend(s, pv):
        m_new = jnp.maximum(m_ref[...], jnp.max(s, axis=1, keepdims=True))
        alpha = jnp.exp(m_ref[...] - m_new)
        p = jnp.exp(s - m_new)
        l_ref[...] = alpha * l_ref[...] + jnp.sum(p, axis=1, keepdims=True)
        acc_ref[...] = alpha * acc_ref[...] + pv(p.astype(BF16))
        m_ref[...] = m_new

    @pl.when(g == 0)
    def _():
        m_ref[...] = jnp.full_like(m_ref, NEG_INF)
        l_ref[...] = jnp.zeros_like(l_ref)
        acc_ref[...] = jnp.zeros_like(acc_ref)
        suf_ref[...] = jnp.zeros_like(suf_ref)
        lf_ref[...] = jnp.zeros_like(lf_ref)
        lane = lax.broadcasted_iota(jnp.int32, (nrow, 128), 1)
        lfmat = lfmat_ref[0]
        cnew_mat = jnp.zeros_like(lfmat)
        for j in range(n_new):
            cnew_mat = cnew_mat + jnp.where(lane >= j, lfmat[:, j:j + 1], 0.0)
        s = _dot(qbd, knew_ref[0]) + (cnew_col - cnew_mat)
        s = jnp.where(lane <= _row_token((nrow, 128)), s, NEG_INF)
        attend(s, lambda p: _dot(p, vnew_ref[0]))

    for ii in range(n_pp):
        page = pt_ref[b, n_pages - 1 - (g * n_pp + ii)]
        for h in range(H_FOX):
            lf_ref[pl.ds(ii * ROWS_PER_TOKEN + h, 1), :] = lfc_ref[0, h, pl.ds(page, 1), :]
    lf_all = lf_ref[...]
    within = _dot_acc3(lf_all, u)
    totals = jnp.sum(lf_all, axis=1, keepdims=True)
    later_pages = suf_ref[...]
    scores = []
    for ii in range(n_pp):
        sl = slice(ii * ROWS_PER_TOKEN, (ii + 1) * ROWS_PER_TOKEN)
        bias = jnp.concatenate([within[sl] + later_pages] * n_new, axis=0) + cnew_col
        kp = k_refs[ii][0, 0].reshape(H_FOX * HEAD_DIM, 128).astype(BF16)
        scores.append(_dot(qbd, kp) + bias)
        later_pages = later_pages + totals[sl]
    suf_ref[...] = later_pages

    def pv(p):
        out = None
        for ii in range(n_pp):
            vp = v_refs[ii][0, 0].reshape(H_FOX * HEAD_DIM, 128).astype(BF16)
            part = _dot_nt(p[:, ii * 128:(ii + 1) * 128], vp)
            out = part if out is None else out + part
        return out

    attend(jnp.concatenate(scores, axis=1), pv)

    @pl.when(g == pl.num_programs(1) - 1)
    def _():
        o_ref[0] = _collapse_heads(acc_ref[...] / l_ref[...], n_new, GROUP_W)


def _pages_per_step(n_pages):
    return min(16, n_pages)


def _fox_decode(layer, page_table, qbd, knew_t, vnew, lfcol, lfmat, cache_k, cache_v, cache_lf):
    nb, nrow, w = qbd.shape
    n_new = nrow // ROWS_PER_TOKEN
    n_pages = page_table.shape[1]
    n_pp = _pages_per_step(n_pages)
    u = (jnp.arange(128)[:, None] > jnp.arange(128)[None, :]).astype(BF16)
    per_seq = lambda a: pl.BlockSpec((1,) + a.shape[1:], lambda b, g, pt: (b,) + (0,) * (a.ndim - 1))

    def page_spec(ii):
        return pl.BlockSpec((1, 1) + cache_k.shape[2:],
                            lambda b, g, pt: (layer, pt[b, n_pages - 1 - (g * n_pp + ii)], 0, 0, 0))

    grid_spec = pltpu.PrefetchScalarGridSpec(
        num_scalar_prefetch=1,
        grid=(nb, n_pages // n_pp),
        in_specs=[per_seq(qbd), per_seq(knew_t), per_seq(vnew), per_seq(lfcol), per_seq(lfmat),
                  pl.BlockSpec((1,) + cache_lf.shape[1:], lambda b, g, pt: (layer, 0, 0, 0)),
                  pl.BlockSpec(u.shape, lambda b, g, pt: (0, 0))]
        + [page_spec(ii) for ii in range(n_pp)] * 2,
        out_specs=pl.BlockSpec((1, n_new, w), lambda b, g, pt: (b, 0, 0)),
        scratch_shapes=[pltpu.VMEM((nrow, 1), F32), pltpu.VMEM((nrow, 1), F32), pltpu.VMEM((nrow, w), F32),
                        pltpu.VMEM((ROWS_PER_TOKEN, 1), F32), pltpu.VMEM((n_pp * ROWS_PER_TOKEN, 128), F32)],
    )
    return pl.pallas_call(
        functools.partial(_fox_decode_kernel, n_pp=n_pp, n_new=n_new, n_pages=n_pages),
        grid_spec=grid_spec,
        out_shape=jax.ShapeDtypeStruct((nb, n_new, w), F32),
        compiler_params=_cparams("parallel", "arbitrary"),
        name="fox_decode",
    )(page_table, qbd, knew_t, vnew, lfcol, lfmat, cache_lf, u, *([cache_k] * n_pp), *([cache_v] * n_pp))


def _sb_decode_kernel(pt_ref, qbd_ref, knew_ref, vnew_ref, u_ref, *rest, n_pp, n_new):
    k_refs = rest[:n_pp]
    v_refs = rest[n_pp:2 * n_pp]
    o_ref = rest[2 * n_pp]
    rest_ref, acc_ref = rest[2 * n_pp + 1:]
    g = pl.program_id(1)
    nrow = n_new * ROWS_PER_TOKEN
    width = H_SB * HEAD_DIM
    qbd = qbd_ref[0]
    u = u_ref[...]

    def weights(z_blocks, valid):
        n = len(z_blocks)
        z = jnp.concatenate(z_blocks, axis=0)
        pr = jnp.maximum(z, jnp.log2(1.0 + jnp.exp2(jnp.minimum(z, EXP2_CLAMP))))
        if valid is not None:
            pr = jnp.where(valid, pr, 0.0)
        upto = _dot(pr.astype(BF16), u)
        run = rest_ref[...]
        out = []
        for ii in range(n):
            sl = slice(ii * nrow, (ii + 1) * nrow)
            a = jnp.exp2(z[sl] + upto[sl] + run)
            if valid is not None:
                a = jnp.where(valid, a, 0.0)
            out.append(a.astype(BF16))
            run = run + upto[sl, 0:1]
        rest_ref[...] = run
        return out

    @pl.when(g == 0)
    def _():
        rest_ref[...] = jnp.zeros_like(rest_ref)
        acc_ref[...] = jnp.zeros_like(acc_ref)
        lane = lax.broadcasted_iota(jnp.int32, (nrow, 128), 1)
        (a,) = weights([_dot(qbd, knew_ref[0])], lane < _row_token((nrow, 128)))
        acc_ref[...] = _dot(a, vnew_ref[0])

    z_blocks = [_dot(qbd, k_refs[ii][0, 0].reshape(width, 128).astype(BF16)) for ii in range(n_pp)]
    a_blocks = weights(z_blocks, None)
    out = acc_ref[...]
    for ii in range(n_pp):
        out = out + _dot_nt(a_blocks[ii], v_refs[ii][0, 0].reshape(width, 128).astype(BF16))
    acc_ref[...] = out

    @pl.when(g == pl.num_programs(1) - 1)
    def _():
        o_ref[0] = _collapse_heads(acc_ref[...], n_new, width)


def _sb_decode(layer, page_table, qbd, knew_t, vnew, cache_k, cache_v):
    nb, nrow, width = qbd.shape
    n_new = nrow // ROWS_PER_TOKEN
    n_pages = page_table.shape[1]
    n_pp = _pages_per_step(n_pages)
    u = -(jnp.arange(128)[:, None] >= jnp.arange(128)[None, :]).astype(BF16)
    per_seq = lambda a: pl.BlockSpec((1,) + a.shape[1:], lambda b, g, pt: (b,) + (0,) * (a.ndim - 1))

    def page_spec(ii):
        return pl.BlockSpec((1, 1) + cache_k.shape[2:],
                            lambda b, g, pt: (layer, pt[b, n_pages - 1 - (g * n_pp + ii)], 0, 0, 0))

    grid_spec = pltpu.PrefetchScalarGridSpec(
        num_scalar_prefetch=1,
        grid=(nb, n_pages // n_pp),
        in_specs=[per_seq(qbd), per_seq(knew_t), per_seq(vnew), pl.BlockSpec(u.shape, lambda b, g, pt: (0, 0))]
        + [page_spec(ii) for ii in range(n_pp)] * 2,
        out_specs=pl.BlockSpec((1, n_new, width), lambda b, g, pt: (b, 0, 0)),
        scratch_shapes=[pltpu.VMEM((nrow, 1), F32), pltpu.VMEM((nrow, width), F32)],
    )
    return pl.pallas_call(
        functools.partial(_sb_decode_kernel, n_pp=n_pp, n_new=n_new),
        grid_spec=grid_spec,
        out_shape=jax.ShapeDtypeStruct((nb, n_new, width), F32),
        compiler_params=_cparams("parallel", "arbitrary"),
        name="sb_decode",
    )(page_table, qbd, knew_t, vnew, u, *([cache_k] * n_pp), *([cache_v] * n_pp))


def _pad_cols(a, width=GROUP_W):
    return jnp.pad(a, ((0, 0), (0, width - a.shape[1])))


def _split_mix_weight(w):
    wf, wr, ws = H_FOX * HEAD_DIM, H_RET * HEAD_DIM, H_SB * HEAD_DIM
    sizes = (wf, wf, wf, H_FOX, wr, wr, wr, wr, ws, ws, ws)
    offs = np.cumsum((0,) + sizes)
    return [w[:, offs[i]:offs[i + 1]] for i in range(len(sizes))]


def _rope_swap_perm(n_heads):
    idx = np.arange(n_heads * HEAD_DIM)
    return (idx // HEAD_DIM) * HEAD_DIM + (idx % HEAD_DIM + HEAD_DIM // 2) % HEAD_DIM


def _rope_tables(pos, n_heads):
    half = HEAD_DIM // 2
    inv_freq = ROPE_BASE ** (-jnp.arange(half, dtype=F32) / half)
    ang = pos.astype(F32)[:, None] * inv_freq[None, :]
    cos, sin = jnp.cos(ang), jnp.sin(ang)
    cos_t = jnp.tile(jnp.concatenate([cos, cos], axis=1), (1, n_heads))
    sin_t = jnp.tile(jnp.concatenate([-sin, sin], axis=1), (1, n_heads))
    return _pad_cols(cos_t), _pad_cols(sin_t)


def _block_diag_queries(q, n_heads):
    nb, n_new, w = q.shape
    slot = jnp.arange(ROWS_PER_TOKEN)[:, None]
    lane_head = (jnp.arange(w) // HEAD_DIM)[None, :]
    mask = (slot == lane_head) & (slot < n_heads)
    qb = jnp.where(mask[None, None], q[:, :, None, :], 0.0)
    return qb.reshape(nb, n_new * ROWS_PER_TOKEN, w).astype(BF16)


def _new_keys_t(k, width):
    nb, n_new, _ = k.shape
    kt = jnp.swapaxes(k[:, :, :width], 1, 2)
    return jnp.pad(kt, ((0, 0), (0, 0), (0, 128 - n_new))).astype(BF16)


def _new_values(v, width):
    nb, n_new, _ = v.shape
    return jnp.pad(v[:, :, :width], ((0, 0), (0, 128 - n_new), (0, 0))).astype(BF16)


def kernel(x_prompt, x_sample, cache_fox_k, cache_fox_v, cache_fox_logf, cache_sb_k, cache_sb_v, state_ret,
           page_table, norm_g, w_ffn1_in, w_ffn1_out, w_mix_in, b_forget, ret_gn_g, w_mix_out, w_ffn2_in,
           w_ffn2_out):
    depth = norm_g.shape[0]
    bsz, seq, d = x_prompt.shape
    nb, n_new, _ = x_sample.shape
    n_pages, page_size = page_table.shape[1], cache_fox_k.shape[2]
    past = n_pages * page_size
    wf, wr, ws = H_FOX * HEAD_DIM, H_RET * HEAD_DIM, H_SB * HEAD_DIM
    w = GROUP_W

    ck_fox = jnp.transpose(cache_fox_k, (0, 1, 3, 4, 2))
    cv_fox = jnp.transpose(cache_fox_v, (0, 1, 3, 4, 2))
    ck_sb = jnp.transpose(cache_sb_k, (0, 1, 3, 4, 2))
    cv_sb = jnp.transpose(cache_sb_v, (0, 1, 3, 4, 2))
    clf = jnp.transpose(cache_fox_logf, (0, 3, 1, 2))

    cos_p, sin_p = _rope_tables(jnp.arange(seq, dtype=jnp.int32), H_RET)
    cos_s, sin_s = _rope_tables(past + jnp.arange(n_new, dtype=jnp.int32), H_RET)
    cos_s, sin_s = jnp.tile(cos_s, (nb, 1)), jnp.tile(sin_s, (nb, 1))
    perm = _rope_swap_perm(H_RET)
    avg = jnp.asarray(np.kron(np.eye(w // HEAD_DIM), np.full((HEAD_DIM, HEAD_DIM), 1.0 / HEAD_DIM)), BF16)

    yp = x_prompt.reshape(bsz * seq, d)
    ys = x_sample.reshape(nb * n_new, d)
    st_p, st_s = [], []
    for l in range(depth):
        g = [norm_g[l, i][None, :] for i in range(norm_g.shape[1])]
        w1_in, w1_out = w_ffn1_in[l].astype(BF16), w_ffn1_out[l].astype(BF16)
        w2_in, w2_out = w_ffn2_in[l].astype(BF16), w_ffn2_out[l].astype(BF16)
        qa, ka, va, fa, qr, kr, vr, gr, qs, ks, vs = _split_mix_weight(w_mix_in[l])
        ret_cols = [_pad_cols(qr), _pad_cols(qr[:, perm]), _pad_cols(kr), _pad_cols(kr[:, perm]),
                    _pad_cols(vr), _pad_cols(gr)]
        kv_cols = [ka, va, _pad_cols(ks), _pad_cols(vs)]
        wrow = jnp.concatenate([ka, _pad_cols(ks)] + ret_cols, axis=1).astype(BF16)
        wt = jnp.concatenate(kv_cols + [qa, _pad_cols(qs), _pad_cols(fa, 16)], axis=1).T.astype(BF16)
        w_samp = jnp.concatenate([qa, _pad_cols(qs)] + ret_cols + kv_cols + [_pad_cols(fa, 128)], axis=1).astype(BF16)
        bf_col = jnp.pad(b_forget[l], (0, 16 - H_FOX))[:, None]
        bf_row = jnp.pad(b_forget[l], (0, 128 - H_FOX))[None, :]
        gn = _pad_cols(ret_gn_g[l][None, :])
        wo = w_mix_out[l]
        wo1 = wo[:wf].astype(BF16)
        wo2 = jnp.pad(wo[wf:wf + wr], ((0, w - wr), (0, 0))).astype(BF16)
        wo3 = jnp.pad(wo[wf + wr:], ((0, w - ws), (0, 0))).astype(BF16)

        hp = _ffn_block(yp, g[0], w1_in, w1_out, g[1])
        (ksr, qr_p, kr_p, vr_p, gr_p, kaug, kat, vat, kst, vst, vstb, qat, qst, lft, qaug, vtaug) = _proj_prompt(
            hp.reshape(bsz, seq, d), g[2], wrow, wt, bf_col, cos_p, sin_p)
        o_fox = _fox_prompt(kaug, qat, qaug, vtaug)
        o_ret, s_raw = _ret_prompt(qr_p, kr_p, vr_p)
        o_sb_a, o_sb_b = _sb_prompt(ksr, qst, vstb)
        hp = _merge_out(hp, o_fox.reshape(bsz * seq, w), o_ret.reshape(bsz * seq, w), gr_p.reshape(bsz * seq, w),
                        o_sb_a.reshape(bsz * seq, -1), o_sb_b.reshape(bsz * seq, -1), gn, avg, wo1, wo2, wo3, g[3])
        yp = _ffn_block(hp, g[4], w2_in, w2_out, g[5])
        s_fin = jnp.stack([s_raw[:, h // 2, h % 2, (h % 2) * HEAD_DIM:(h % 2 + 1) * HEAD_DIM,
                                 (h % 2) * HEAD_DIM:(h % 2 + 1) * HEAD_DIM] for h in range(H_RET)], axis=1)
        to_heads = lambda a, nh: jnp.transpose(a.reshape(bsz, nh, HEAD_DIM, seq), (0, 3, 1, 2))
        st_p.append((to_heads(kat, H_FOX), to_heads(vat, H_FOX), jnp.transpose(lft[:, :H_FOX], (0, 2, 1)),
                     to_heads(kst, H_SB), to_heads(vst, H_SB), s_fin))

        hs = _ffn_block(ys, g[0], w1_in, w1_out, g[1])
        (qa_s, qs_s, qr_s, kr_s, vr_s, gr_s, ka_s, va_s, ks_s, vs_s, lf_s) = _proj_sample(
            hs, g[2], w_samp, bf_row, cos_s, sin_s)
        seqs = lambda a: a.reshape(nb, n_new, a.shape[-1])
        lf_new = seqs(lf_s)[:, :, :ROWS_PER_TOKEN]
        lfcol = lf_new.reshape(nb, n_new * ROWS_PER_TOKEN, 1)
        lfmat = jnp.broadcast_to(jnp.transpose(lf_new, (0, 2, 1))[:, None], (nb, n_new, ROWS_PER_TOKEN, n_new))
        lfmat = jnp.pad(lfmat.reshape(nb, n_new * ROWS_PER_TOKEN, n_new), ((0, 0), (0, 0), (0, 128 - n_new)))
        o_fox_s = _fox_decode(l, page_table, _block_diag_queries(seqs(qa_s), H_FOX), _new_keys_t(seqs(ka_s), wf),
                              _new_values(seqs(va_s), wf), lfcol, lfmat, ck_fox, cv_fox, clf)
        o_sb_s = _sb_decode(l, page_table, _block_diag_queries(seqs(qs_s)[:, :, :ws] * LOG2E, H_SB),
                            _new_keys_t(seqs(ks_s), ws), _new_values(seqs(vs_s), ws), ck_sb, cv_sb)
        o_ret_s, s_new = _ret_sample(seqs(qr_s), seqs(kr_s), seqs(vr_s), state_ret[l])
        o_sb_rows = _pad_cols(o_sb_s.reshape(nb * n_new, ws)).astype(BF16)
        hs = _merge_out(hs, o_fox_s.reshape(nb * n_new, w).astype(BF16), o_ret_s.reshape(nb * n_new, w), gr_s,
                        o_sb_rows[:, :o_sb_a.shape[-1]], o_sb_rows[:, o_sb_a.shape[-1]:], gn, avg, wo1, wo2, wo3, g[3])
        ys = _ffn_block(hs, g[4], w2_in, w2_out, g[5])
        heads = lambda a, nh: a[:, :nh * HEAD_DIM].reshape(nb, n_new, nh, HEAD_DIM)
        st_s.append((heads(ka_s, H_FOX), heads(va_s, H_FOX), seqs(lf_s)[:, :, :H_FOX],
                     heads(ks_s, H_SB), heads(vs_s, H_SB), s_new))

    stk = lambda sts, i: jnp.stack([s[i] for s in sts], axis=0)
    return (yp.reshape(bsz, seq, d), ys.reshape(nb, n_new, d),
            stk(st_p, 0), stk(st_p, 1), stk(st_p, 2), stk(st_p, 3), stk(st_p, 4), stk(st_p, 5),
            stk(st_s, 0), stk(st_s, 1), stk(st_s, 2), stk(st_s, 3), stk(st_s, 4), stk(st_s, 5))
```

```python
import functools
import math

import numpy as np
import jax
import jax.numpy as jnp
from jax import lax
from jax.experimental import pallas as pl
from jax.experimental.pallas import tpu as pltpu

HEAD_DIM = 64
H_FOX = 6
H_RET = 5
H_SB = 5
GROUP_W = 384
HEADS_PER_BLOCK = 2
EPS = 1e-6
ROPE_BASE = 10000.0
Q_SCALE = HEAD_DIM ** -0.5
LOG2E = math.log2(math.e)
AUG_ROWS = 16
V_ROWS = 80
CUM_BLOCK = 256
EXP2_CLAMP = 64.0
FF_CHUNK = 256
RET_CHUNK = 512
PROMPT_TQ = 1024
FOX_TK = 1024
SB_TK = 1024
VMEM_LIMIT_BYTES = 48 * 1024 * 1024
F32 = jnp.float32
BF16 = jnp.bfloat16
NEG_INF = float("-inf")


def _cparams(*sem):
    return pltpu.CompilerParams(dimension_semantics=sem, vmem_limit_bytes=VMEM_LIMIT_BYTES)


def _dot(a, b):
    return jnp.dot(a, b, preferred_element_type=F32)


def _dot_nt(a, b):
    return lax.dot_general(a, b, (((1,), (1,)), ((), ())), preferred_element_type=F32)


def _dot_tn(a, b):
    return lax.dot_general(a, b, (((0,), (0,)), ((), ())), preferred_element_type=F32)


def _split3(x):
    hi = x.astype(BF16)
    r = x - hi.astype(F32)
    mid = r.astype(BF16)
    lo = (r - mid.astype(F32)).astype(BF16)
    return hi, mid, lo


def _dot_acc3(x, m):
    hi, mid, lo = _split3(x)
    return _dot(hi, m) + _dot(mid, m) + _dot(lo, m)


def _rms(x, g):
    return x * lax.rsqrt(jnp.mean(x * x, axis=-1, keepdims=True) + EPS) * g


def _softplus_parts(z):
    lp = jnp.log1p(jnp.exp(-jnp.abs(z)))
    return -jnp.maximum(z, 0.0) - lp, jnp.minimum(z, 0.0) - lp


def _ffn_rows(x, gpre_ref, wg_ref, wu_ref, wo_ref, gpost_ref):
    xn = _rms(x, gpre_ref[...]).astype(BF16)
    hidden = []
    for c in range(wg_ref.shape[1] // FF_CHUNK):
        sl = slice(c * FF_CHUNK, (c + 1) * FF_CHUNK)
        gate = _dot(xn, wg_ref[:, sl])
        up = _dot(xn, wu_ref[:, sl])
        hidden.append((gate * jax.nn.sigmoid(gate) * up).astype(BF16))
    y = _dot(jnp.concatenate(hidden, axis=1), wo_ref[...])
    return x + 0.5 * _rms(y, gpost_ref[...])


def _ffn_kernel(x_ref, gpre_ref, wg_ref, wu_ref, wo_ref, gpost_ref, o_ref):
    o_ref[...] = _ffn_rows(x_ref[...], gpre_ref, wg_ref, wu_ref, wo_ref, gpost_ref)


def _ffn_block(x, g_pre, w_in, w_out, g_post):
    m, d = x.shape
    dff = w_out.shape[0]
    tm = min(512, m)
    once = pl.Buffered(1)
    return pl.pallas_call(
        _ffn_kernel,
        grid=(m // tm,),
        in_specs=[
            pl.BlockSpec((tm, d), lambda i: (i, 0)),
            pl.BlockSpec((1, d), lambda i: (0, 0)),
            pl.BlockSpec((d, dff), lambda i: (0, 0), pipeline_mode=once),
            pl.BlockSpec((d, dff), lambda i: (0, 1), pipeline_mode=once),
            pl.BlockSpec((dff, d), lambda i: (0, 0), pipeline_mode=once),
            pl.BlockSpec((1, d), lambda i: (0, 0)),
        ],
        out_specs=pl.BlockSpec((tm, d), lambda i: (i, 0)),
        out_shape=jax.ShapeDtypeStruct((m, d), F32),
        compiler_params=_cparams("parallel"),
        name="ffn_block",
    )(x, g_pre, w_in, w_in, w_out, g_post)


def _proj_prompt_kernel(x_ref, g_ref, wrow_ref, wt_ref, bf_ref, cos_ref, sin_ref, tri_ref, kplace_ref, qplace_ref,
                        ksr_ref, qr_ref, kr_ref, vr_ref, gr_ref, kaug_ref,
                        kat_ref, vat_ref, kst_ref, vst_ref,
                        vstb_ref, qat_ref, qst_ref,
                        lf_ref, qaug_ref, vtaug_ref, carry_ref):
    w = GROUP_W
    t = pl.program_id(1)

    @pl.when(t == 0)
    def _():
        carry_ref[...] = jnp.zeros_like(carry_ref)

    xn = _rms(x_ref[0], g_ref[...]).astype(BF16)
    pr = _dot(xn, wrow_ref[...])
    cos = cos_ref[...]
    sin = sin_ref[...]
    ksr_ref[0] = pr[:, w:2 * w].astype(BF16)
    qr_ref[0] = (pr[:, 2 * w:3 * w] * cos + pr[:, 3 * w:4 * w] * sin).astype(BF16)
    kr_ref[0] = ((pr[:, 4 * w:5 * w] * cos + pr[:, 5 * w:6 * w] * sin) * Q_SCALE).astype(BF16)
    vr_ref[0] = pr[:, 6 * w:7 * w].astype(BF16)
    gr_ref[0] = pr[:, 7 * w:8 * w]

    pt = _dot_nt(wt_ref[...], xn)
    kat = pt[0:w]
    vat = pt[w:2 * w]
    kst = pt[2 * w:3 * w]
    vst = pt[3 * w:4 * w]
    kat_ref[0] = kat
    vat_ref[0] = vat
    kst_ref[0] = kst[0:H_SB * HEAD_DIM]
    vst_ref[0] = vst[0:H_SB * HEAD_DIM]
    vstb_ref[0] = vst.astype(BF16)
    qat_ref[0] = (pt[4 * w:5 * w] * (Q_SCALE * LOG2E)).astype(BF16)
    qst_ref[0] = (pt[5 * w:6 * w] * (Q_SCALE * LOG2E)).astype(BF16)

    logf = jax.nn.log_sigmoid(pt[6 * w:6 * w + 16] + bf_ref[...])
    lf_ref[0] = logf
    c = _dot_acc3(logf, tri_ref[...]) + carry_ref[...]
    carry_ref[...] = c[:, -1:]

    tm = c.shape[1]
    one_row = jnp.concatenate([jnp.ones((1, tm), BF16), jnp.zeros((15, tm), BF16)], axis=0)
    csplit = jnp.concatenate(list(_split3(c * LOG2E)) + [one_row], axis=0)
    for p in range(w // 128):
        kaug_ref[0, p, :, 0:128] = pr[:, p * 128:(p + 1) * 128].astype(BF16)
        kaug_ref[0, p, :, 128:256] = _dot_tn(csplit, kplace_ref[p]).astype(BF16)
    q_rows = _dot(qplace_ref[...], csplit).astype(BF16)
    for h in range(H_FOX):
        qaug_ref[0, h] = q_rows[h * AUG_ROWS:(h + 1) * AUG_ROWS]
        vtaug_ref[0, h, 0:HEAD_DIM, :] = vat[h * HEAD_DIM:(h + 1) * HEAD_DIM].astype(BF16)
        vtaug_ref[0, h, HEAD_DIM:V_ROWS, :] = one_row


def _fox_placements():
    nb = GROUP_W // 128
    kplace = np.zeros((nb, 64, 128), np.float32)
    qplace = np.zeros((H_FOX * AUG_ROWS, 64), np.float32)
    for h in range(H_FOX):
        p, hh = divmod(h, HEADS_PER_BLOCK)
        for sp in range(3):
            kplace[p, sp * 16 + h, hh * 3 + sp] = -1.0
            kplace[p, 48, HEADS_PER_BLOCK * 3 + sp] = 1.0
            qplace[h * AUG_ROWS + hh * 3 + sp, 48] = 1.0
            qplace[h * AUG_ROWS + HEADS_PER_BLOCK * 3 + sp, sp * 16 + h] = 1.0
    return jnp.asarray(kplace, BF16), jnp.asarray(qplace, BF16)


def _proj_prompt(x, g, wrow, wt, bf, cos, sin):
    b, t, d = x.shape
    tm = min(512, t)
    w = GROUP_W
    nb = w // 128
    tri = (jnp.arange(tm)[:, None] <= jnp.arange(tm)[None, :]).astype(BF16)
    kplace, qplace = _fox_placements()
    row = lambda dt: jax.ShapeDtypeStruct((b, t, w), dt)
    col = lambda n, dt: jax.ShapeDtypeStruct((b, n, t), dt)
    rspec = pl.BlockSpec((1, tm, w), lambda bi, ti: (bi, ti, 0))
    cspec = lambda n: pl.BlockSpec((1, n, tm), lambda bi, ti: (bi, 0, ti))
    const = lambda a: pl.BlockSpec(a.shape, lambda bi, ti: (0,) * a.ndim)
    nsb = H_SB * HEAD_DIM
    return pl.pallas_call(
        _proj_prompt_kernel,
        grid=(b, t // tm),
        in_specs=[
            pl.BlockSpec((1, tm, d), lambda bi, ti: (bi, ti, 0)),
            const(g), const(wrow), const(wt), const(bf),
            pl.BlockSpec((tm, w), lambda bi, ti: (ti, 0)),
            pl.BlockSpec((tm, w), lambda bi, ti: (ti, 0)),
            const(tri), const(kplace), const(qplace),
        ],
        out_specs=[rspec] * 5 + [pl.BlockSpec((1, nb, tm, 256), lambda bi, ti: (bi, 0, ti, 0))]
        + [cspec(w), cspec(w), cspec(nsb), cspec(nsb)] + [cspec(w)] * 3 + [cspec(16)]
        + [pl.BlockSpec((1, H_FOX, AUG_ROWS, tm), lambda bi, ti: (bi, 0, 0, ti)),
           pl.BlockSpec((1, H_FOX, V_ROWS, tm), lambda bi, ti: (bi, 0, 0, ti))],
        out_shape=[row(BF16)] * 4 + [row(F32), jax.ShapeDtypeStruct((b, nb, t, 256), BF16)]
        + [col(w, F32), col(w, F32), col(nsb, F32), col(nsb, F32)] + [col(w, BF16)] * 3 + [col(16, F32)]
        + [jax.ShapeDtypeStruct((b, H_FOX, AUG_ROWS, t), BF16), jax.ShapeDtypeStruct((b, H_FOX, V_ROWS, t), BF16)],
        scratch_shapes=[pltpu.VMEM((16, 1), F32)],
        compiler_params=_cparams("parallel", "arbitrary"),
        name="proj_prompt",
    )(x, g, wrow, wt, bf, cos, sin, tri, kplace, qplace)


def _proj_sample_kernel(x_ref, g_ref, w_ref, bf_ref, cos_ref, sin_ref,
                        qa_ref, qs_ref, qr_ref, kr_ref, vr_ref, gr_ref,
                        ka_ref, va_ref, ks_ref, vs_ref, lf_ref):
    w = GROUP_W
    xn = _rms(x_ref[...], g_ref[...]).astype(BF16)
    pr = _dot(xn, w_ref[...])
    cos = cos_ref[...]
    sin = sin_ref[...]
    qa_ref[...] = pr[:, 0:w] * Q_SCALE
    qs_ref[...] = pr[:, w:2 * w] * Q_SCALE
    qr_ref[...] = pr[:, 2 * w:3 * w] * cos + pr[:, 3 * w:4 * w] * sin
    kr_ref[...] = (pr[:, 4 * w:5 * w] * cos + pr[:, 5 * w:6 * w] * sin) * Q_SCALE
    vr_ref[...] = pr[:, 6 * w:7 * w]
    gr_ref[...] = pr[:, 7 * w:8 * w]
    ka_ref[...] = pr[:, 8 * w:9 * w]
    va_ref[...] = pr[:, 9 * w:10 * w]
    ks_ref[...] = pr[:, 10 * w:11 * w]
    vs_ref[...] = pr[:, 11 * w:12 * w]
    lf_ref[...] = jax.nn.log_sigmoid(pr[:, 12 * w:12 * w + 128] + bf_ref[...])


def _proj_sample(x, g, w_all, bf_row, cos, sin):
    m = x.shape[0]
    w = GROUP_W
    return pl.pallas_call(
        _proj_sample_kernel,
        out_shape=[jax.ShapeDtypeStruct((m, w), F32)] * 10 + [jax.ShapeDtypeStruct((m, 128), F32)],
        compiler_params=pltpu.CompilerParams(vmem_limit_bytes=VMEM_LIMIT_BYTES),
        name="proj_sample",
    )(x, g, w_all, bf_row, cos, sin)


def _merge_ffn_kernel(h_ref, of_ref, oret_ref, gr_ref, osba_ref, osbb_ref, gn_ref, avg_ref,
                      w1_ref, w2_ref, w3a_ref, w3b_ref, g_ref,
                      gpre_ref, wg_ref, wu_ref, wo_ref, gpost_ref, o_ref):
    x = oret_ref[...]
    avg = avg_ref[...]
    hi, mid, lo = _split3(x)
    mu = _dot(hi, avg) + _dot(mid, avg) + _dot(lo, avg)
    dev = x - mu
    var = _dot_acc3(dev * dev, avg)
    gate = gr_ref[...]
    r = dev * lax.rsqrt(var + EPS) * gn_ref[...] * (gate * jax.nn.sigmoid(gate))
    y = (_dot(of_ref[...], w1_ref[...]) + _dot(r.astype(BF16), w2_ref[...])
         + _dot(osba_ref[...], w3a_ref[...]) + _dot(osbb_ref[...], w3b_ref[...]))
    mixed = h_ref[...] + _rms(y, g_ref[...])
    o_ref[...] = _ffn_rows(mixed, gpre_ref, wg_ref, wu_ref, wo_ref, gpost_ref)


def _merge_ffn(h, o_fox, o_ret, gr, o_sb_a, o_sb_b, gn, avg, w1, w2, w3, g, g_pre, w_in, w_out, g_post):
    m, d = h.shape
    w = GROUP_W
    dff = w_out.shape[0]
    wa = o_sb_a.shape[1]
    w3a, w3b = w3[:wa], w3[wa:]
    tm = min(512, m)
    once = pl.Buffered(1)
    rows = lambda n: pl.BlockSpec((tm, n), lambda i: (i, 0))
    const = lambda a: pl.BlockSpec(a.shape, lambda i: (0,) * a.ndim)
    return pl.pallas_call(
        _merge_ffn_kernel,
        grid=(m // tm,),
        in_specs=[rows(d), rows(w), rows(w), rows(w), rows(wa), rows(w - wa), const(gn), const(avg),
                  const(w1), const(w2), const(w3a), const(w3b), const(g), const(g_pre),
                  pl.BlockSpec((d, dff), lambda i: (0, 0), pipeline_mode=once),
                  pl.BlockSpec((d, dff), lambda i: (0, 1), pipeline_mode=once),
                  pl.BlockSpec((dff, d), lambda i: (0, 0), pipeline_mode=once),
                  const(g_post)],
        out_specs=rows(d),
        out_shape=jax.ShapeDtypeStruct((m, d), F32),
        compiler_params=_cparams("parallel"),
        name="merge_ffn",
    )(h, o_fox, o_ret, gr, o_sb_a, o_sb_b, gn, avg, w1, w2, w3a, w3b, g, g_pre, w_in, w_in, w_out, g_post)


def _head_lane_mask(shape, hh):
    lane = lax.broadcasted_iota(jnp.int32, shape, len(shape) - 1)
    return (lane >= HEAD_DIM * hh) & (lane < HEAD_DIM * (hh + 1))


def _head_sublane_mask(shape, hh):
    r = lax.broadcasted_iota(jnp.int32, shape, 0)
    return (r >= HEAD_DIM * hh) & (r < HEAD_DIM * (hh + 1))


def _fox_prompt_kernel(kaug_ref, qt_ref, qaug_ref, vt_ref, o_ref, qa_ref, *, tq, tk):
    i = pl.program_id(2)
    qt2 = qt_ref[0]
    kdim = kaug_ref.shape[-1]
    zpad = jnp.zeros((kdim - 128 - AUG_ROWS, tq), BF16)
    for hh in range(HEADS_PER_BLOCK):
        qa_ref[hh] = jnp.concatenate(
            [jnp.where(_head_sublane_mask(qt2.shape, hh), qt2, jnp.zeros_like(qt2)), qaug_ref[0, hh], zpad], axis=0)
    krow = lax.broadcasted_iota(jnp.int32, (tk, tq), 0)
    qcol = lax.broadcasted_iota(jnp.int32, (tk, tq), 1)
    sub = tq // tk

    def step(j, carry, diag):
        off = pl.multiple_of(j * tk, tk)
        kt = kaug_ref[0, 0, pl.ds(off, tk), :]
        new = []
        for hh in range(HEADS_PER_BLOCK):
            m, acc = carry[hh]
            s = _dot(kt, qa_ref[hh])
            if diag is not None:
                s = jnp.where(krow + diag * tk <= qcol, s, NEG_INF)
            m_new = jnp.maximum(m, jnp.max(s, axis=0, keepdims=True))
            alpha = jnp.exp2(m - m_new)
            p = jnp.exp2(s - m_new).astype(BF16)
            vt = vt_ref[0, hh, :, pl.ds(off, tk)]
            new.append((m_new, alpha * acc + _dot(vt, p)))
        return tuple(new)

    init = tuple((jnp.full((1, tq), NEG_INF, F32), jnp.zeros((V_ROWS, tq), F32)) for _ in range(HEADS_PER_BLOCK))
    carry = lax.fori_loop(0, i * sub, functools.partial(step, diag=None), init)
    for d in range(sub):
        carry = step(i * sub + d, carry, d)
    o_t = jnp.concatenate([acc[0:HEAD_DIM] / acc[HEAD_DIM:HEAD_DIM + 1] for _, acc in carry], axis=0)
    o_ref[0] = o_t.T.astype(o_ref.dtype)


def _fox_prompt(kaug, qt, qaug, vtaug):
    b, nb, t, kdim = kaug.shape
    tq = min(PROMPT_TQ, t)
    tk = min(FOX_TK, tq)
    return pl.pallas_call(
        functools.partial(_fox_prompt_kernel, tq=tq, tk=tk),
        grid=(b, nb, t // tq),
        in_specs=[
            pl.BlockSpec((1, 1, t, kdim), lambda bi, p, i: (bi, p, 0, 0)),
            pl.BlockSpec((1, 128, tq), lambda bi, p, i: (bi, p, i)),
            pl.BlockSpec((1, HEADS_PER_BLOCK, AUG_ROWS, tq), lambda bi, p, i: (bi, p, 0, i)),
            pl.BlockSpec((1, HEADS_PER_BLOCK, V_ROWS, t), lambda bi, p, i: (bi, p, 0, 0)),
        ],
        out_specs=pl.BlockSpec((1, tq, 128), lambda bi, p, i: (bi, i, p)),
        out_shape=jax.ShapeDtypeStruct((b, t, nb * 128), BF16),
        scratch_shapes=[pltpu.VMEM((HEADS_PER_BLOCK, kdim, tq), BF16)],
        compiler_params=_cparams("parallel", "parallel", "arbitrary"),
        name="fox_prompt",
    )(kaug, qt, qaug, vtaug)


def _sb_prompt_kernel(k_ref, qt_ref, vt_ref, ut_ref, o_ref, q_ref, *, tq, tk, chains):
    i = pl.program_id(2)
    for c, (bb, hh) in enumerate(chains):
        qt2 = qt_ref[bb]
        q_ref[c] = jnp.where(_head_sublane_mask(qt2.shape, hh), qt2, jnp.zeros_like(qt2))
    ut = ut_ref[...]
    cb = ut.shape[0]
    nsub = tk // cb
    krow = lax.broadcasted_iota(jnp.int32, (tk, tq), 0)
    qcol = lax.broadcasted_iota(jnp.int32, (tk, tq), 1)
    sub = tq // tk

    def step(j, carry, diag):
        off = pl.multiple_of(j * tk, tk)
        new = []
        for c, (bb, hh) in enumerate(chains):
            rest, acc = carry[c]
            z = _dot(k_ref[bb, pl.ds(off, tk), :], q_ref[c])
            pr = jnp.maximum(z, jnp.log2(1.0 + jnp.exp2(jnp.minimum(z, EXP2_CLAMP))))
            if diag is not None:
                valid = krow + diag * tk < qcol
                pr = jnp.where(valid, pr, 0.0)
            blocks = [None] * nsub
            for blk in reversed(range(nsub)):
                sl = slice(blk * cb, (blk + 1) * cb)
                upto = _dot(ut, pr[sl].astype(BF16))
                blocks[blk] = jnp.exp2(z[sl] + upto + rest)
                rest = rest + upto[0:1]
            a = blocks[0] if nsub == 1 else jnp.concatenate(blocks, axis=0)
            if diag is not None:
                a = jnp.where(valid, a, 0.0)
            vt = vt_ref[bb, HEAD_DIM * hh:HEAD_DIM * (hh + 1), pl.ds(off, tk)]
            new.append((rest, acc + _dot(vt, a.astype(BF16))))
        return tuple(new)

    carry = tuple((jnp.zeros((1, tq), F32), jnp.zeros((HEAD_DIM, tq), F32)) for _ in chains)
    for d in reversed(range(sub)):
        carry = step(i * sub + d, carry, d)
    final = lax.fori_loop(0, i * sub, lambda n, c: step(i * sub - 1 - n, c, None), carry)
    for bb in range(o_ref.shape[0]):
        rows = [jnp.zeros((HEAD_DIM, tq), F32)] * HEADS_PER_BLOCK
        for c, (cb_, hh) in enumerate(chains):
            if cb_ == bb:
                rows[hh] = final[c][1]
        o_ref[bb] = jnp.concatenate(rows, axis=0).T.astype(o_ref.dtype)


def _sb_prompt_call(krows, qt, vt, *, first_pair, n_pairs, batch_block, chains):
    b, t, _ = krows.shape
    tq = min(PROMPT_TQ, t)
    tk = min(SB_TK, tq)
    cb = min(CUM_BLOCK, tk)
    ut = -(jnp.arange(cb)[None, :] >= jnp.arange(cb)[:, None]).astype(BF16)
    nbb = batch_block
    return pl.pallas_call(
        functools.partial(_sb_prompt_kernel, tq=tq, tk=tk, chains=chains),
        grid=(b // nbb, n_pairs, t // tq),
        in_specs=[
            pl.BlockSpec((nbb, t, 128), lambda bi, p, i: (bi, 0, first_pair + p)),
            pl.BlockSpec((nbb, 128, tq), lambda bi, p, i: (bi, first_pair + p, i)),
            pl.BlockSpec((nbb, 128, t), lambda bi, p, i: (bi, first_pair + p, 0)),
            pl.BlockSpec((cb, cb), lambda bi, p, i: (0, 0)),
        ],
        out_specs=pl.BlockSpec((nbb, tq, 128), lambda bi, p, i: (bi, i, p)),
        out_shape=jax.ShapeDtypeStruct((b, t, n_pairs * 128), BF16),
        scratch_shapes=[pltpu.VMEM((len(chains), 128, tq), BF16)],
        compiler_params=_cparams("parallel", "parallel", "arbitrary"),
        name="sb_prompt",
    )(krows, qt, vt, ut)


def _sb_prompt(krows, qt, vt):
    b = krows.shape[0]
    full, left = divmod(H_SB, HEADS_PER_BLOCK)
    outs = [_sb_prompt_call(krows, qt, vt, first_pair=0, n_pairs=full, batch_block=1,
                            chains=tuple((0, hh) for hh in range(HEADS_PER_BLOCK)))]
    if left:
        nbb = 2 if b % 2 == 0 else 1
        outs.append(_sb_prompt_call(krows, qt, vt, first_pair=full, n_pairs=1, batch_block=nbb,
                                    chains=tuple((bb, 0) for bb in range(nbb))))
    return outs


def _log_gamma(head):
    pw = jnp.zeros(head.shape, F32)
    for h in range(HEADS_PER_BLOCK * (GROUP_W // 128)):
        pw = jnp.where(head == h, 2.0 ** (-5.0 - h), pw)
    return jnp.log(1.0 - pw)


def _ret_prompt_kernel(q_ref, k_ref, v_ref, o_ref, s_ref, state_ref, dec_ref, qkd_ref, *, ch):
    p = pl.program_id(1)
    c = pl.program_id(2)

    @pl.when(c == 0)
    def _():
        state_ref[...] = jnp.zeros_like(state_ref)
        ii = lax.broadcasted_iota(jnp.int32, (ch, ch), 0)
        jj = lax.broadcasted_iota(jnp.int32, (ch, ch), 1)
        diff = ii - jj
        pos = lax.broadcasted_iota(jnp.int32, (ch, 1), 0).astype(F32)
        for hh in range(HEADS_PER_BLOCK):
            lg = _log_gamma(jnp.full((1, 1), HEADS_PER_BLOCK * p + hh, jnp.int32))
            dec_ref[hh] = jnp.where(diff >= 0, jnp.exp(jnp.maximum(diff, 0).astype(F32) * lg), 0.0)
            qkd_ref[hh, :, 0:1] = jnp.exp((pos + 1.0) * lg)
            qkd_ref[hh, :, 1:2] = jnp.exp((ch - 1.0 - pos) * lg)
            qkd_ref[hh, :, 2:3] = jnp.broadcast_to(jnp.exp(ch * lg), (ch, 1))

    q2 = q_ref[0]
    k2 = k_ref[0]
    v2 = v_ref[0]
    outs = []
    for hh in range(HEADS_PER_BLOCK):
        qh = jnp.where(_head_lane_mask(q2.shape, hh), q2, jnp.zeros_like(q2))
        scores = _dot_nt(qh, k2) * dec_ref[hh]
        inner = _dot(scores.astype(BF16), v2)
        state = state_ref[hh]
        q_dec = (qh.astype(F32) * qkd_ref[hh, :, 0:1]).astype(BF16)
        cross = _dot(q_dec, state.astype(BF16))
        outs.append(inner + cross)
        k_dec = (k2.astype(F32) * qkd_ref[hh, :, 1:2]).astype(BF16)
        state_ref[hh] = qkd_ref[hh, 0:1, 2:3] * state + _dot_tn(k_dec, v2)
    o_ref[0] = jnp.where(_head_lane_mask(outs[0].shape, 0), outs[0], outs[1])

    @pl.when(c == pl.num_programs(2) - 1)
    def _():
        s_ref[0, 0] = state_ref[...]


def _ret_prompt(q, k, v):
    b, t, w = q.shape
    ch = min(RET_CHUNK, t)
    nb = w // 128
    tile = pl.BlockSpec((1, ch, 128), lambda bi, p, c: (bi, c, p))
    return pl.pallas_call(
        functools.partial(_ret_prompt_kernel, ch=ch),
        grid=(b, nb, t // ch),
        in_specs=[tile, tile, tile],
        out_specs=[tile, pl.BlockSpec((1, 1, HEADS_PER_BLOCK, 128, 128), lambda bi, p, c: (bi, p, 0, 0, 0))],
        out_shape=[jax.ShapeDtypeStruct((b, t, w), F32),
                   jax.ShapeDtypeStruct((b, nb, HEADS_PER_BLOCK, 128, 128), F32)],
        scratch_shapes=[pltpu.VMEM((HEADS_PER_BLOCK, 128, 128), F32),
                        pltpu.VMEM((HEADS_PER_BLOCK, ch, ch), F32),
                        pltpu.VMEM((HEADS_PER_BLOCK, ch, 3), F32)],
        compiler_params=_cparams("parallel", "parallel", "arbitrary"),
        name="ret_prompt",
    )(q, k, v)


def _ret_sample_kernel(q_ref, k_ref, v_ref, s_ref, o_ref, snew_ref, *, n_new):
    q = q_ref[0]
    k = k_ref[0]
    v = v_ref[0]
    ii = lax.broadcasted_iota(jnp.int32, (n_new, n_new), 0)
    jj = lax.broadcasted_iota(jnp.int32, (n_new, n_new), 1)
    diff = ii - jj
    pos = lax.broadcasted_iota(jnp.int32, (n_new, 1), 0).astype(F32)
    o_ref[0] = jnp.zeros(o_ref.shape[1:], F32)
    for h in range(H_RET):
        lg = _log_gamma(jnp.full((1, 1), h, jnp.int32))
        sl = slice(h * HEAD_DIM, (h + 1) * HEAD_DIM)
        qh = q[:, sl]
        kh = k[:, sl]
        vh = v[:, sl].astype(BF16)
        decay = jnp.where(diff >= 0, jnp.exp(jnp.maximum(diff, 0).astype(F32) * lg), 0.0)
        scores = _dot_nt(qh.astype(BF16), kh.astype(BF16)) * decay
        inner = _dot(scores.astype(BF16), vh)
        state = s_ref[0, h]
        cross = _dot((qh * jnp.exp((pos + 1.0) * lg)).astype(BF16), state.astype(BF16))
        o_ref[0, :, sl] = inner + cross
        k_dec = (kh * jnp.exp((n_new - 1.0 - pos) * lg)).astype(BF16)
        snew_ref[0, h] = jnp.exp(n_new * lg) * state + _dot_tn(k_dec, vh)


def _ret_sample(q, k, v, state):
    nb, n_new, w = q.shape
    tok = pl.BlockSpec((1, n_new, w), lambda i: (i, 0, 0))
    st = pl.BlockSpec((1, H_RET, HEAD_DIM, HEAD_DIM), lambda i: (i, 0, 0, 0))
    return pl.pallas_call(
        functools.partial(_ret_sample_kernel, n_new=n_new),
        grid=(nb,),
        in_specs=[tok, tok, tok, st],
        out_specs=[tok, st],
        out_shape=[jax.ShapeDtypeStruct((nb, n_new, w), F32),
                   jax.ShapeDtypeStruct(state.shape, F32)],
        compiler_params=_cparams("parallel"),
        name="ret_sample",
    )(q, k, v, state)


ROWS_PER_TOKEN = 8


def _row_token(shape):
    return lax.broadcasted_iota(jnp.int32, shape, 0) // ROWS_PER_TOKEN


def _collapse_heads(acc, n_new, width):
    r = lax.broadcasted_iota(jnp.int32, acc.shape, 0) % ROWS_PER_TOKEN
    lane_head = lax.broadcasted_iota(jnp.int32, acc.shape, 1) // HEAD_DIM
    kept = jnp.where(r == lane_head, acc, 0.0)
    return jnp.sum(kept.reshape(n_new, ROWS_PER_TOKEN, width), axis=1)


def _fox_decode_kernel(pt_ref, qbd_ref, knew_ref, vnew_ref, lfcol_ref, lfmat_ref, lfc_ref, u_ref, *rest,
                       n_pp, n_new, n_pages):
    k_refs = rest[:n_pp]
    v_refs = rest[n_pp:2 * n_pp]
    o_ref = rest[2 * n_pp]
    m_ref, l_ref, acc_ref, suf_ref, lf_ref = rest[2 * n_pp + 1:]
    b = pl.program_id(0)
    g = pl.program_id(1)
    nrow = n_new * ROWS_PER_TOKEN
    qbd = qbd_ref[0]
    u = u_ref[...]

    lfcol = lfcol_ref[0]
    cnew_col = jnp.concatenate(
        [sum(lfcol[j * ROWS_PER_TOKEN:(j + 1) * ROWS_PER_TOKEN] for j in range(t + 1)) for t in range(n_new)], axis=0)

    def attend(s, pv):
        m_new = jnp.maximum(m_ref[...], jnp.max(s, axis=1, keepdims=True))
        alpha = jnp.exp(m_ref[...] - m_new)
        p = jnp.exp(s - m_new)
        l_ref[...] = alpha * l_ref[...] + jnp.sum(p, axis=1, keepdims=True)
        acc_ref[...] = alpha * acc_ref[...] + pv(p.astype(BF16))
        m_ref[...] = m_new

    @pl.when(g == 0)
    def _():
        m_ref[...] = jnp.full_like(m_ref, NEG_INF)
        l_ref[...] = jnp.zeros_like(l_ref)
        acc_ref[...] = jnp.zeros_like(acc_ref)
        suf_ref[...] = jnp.zeros_like(suf_ref)
        lf_ref[...] = jnp.zeros_like(lf_ref)
        lane = lax.broadcasted_iota(jnp.int32, (nrow, 128), 1)
        lfmat = lfmat_ref[0]
        cnew_mat = jnp.zeros_like(lfmat)
        for j in range(n_new):
            cnew_mat = cnew_mat + jnp.where(lane >= j, lfmat[:, j:j + 1], 0.0)
        s = _dot(qbd, knew_ref[0]) + (cnew_col - cnew_mat)
        s = jnp.where(lane <= _row_token((nrow, 128)), s, NEG_INF)
        attend(s, lambda p: _dot(p, vnew_ref[0]))

    for ii in range(n_pp):
        page = pt_ref[b, n_pages - 1 - (g * n_pp + ii)]
        for h in range(H_FOX):
            lf_ref[pl.ds(ii * ROWS_PER_TOKEN + h, 1), :] = lfc_ref[0, h, pl.ds(page, 1), :]
    lf_all = lf_ref[...]
    within = _dot_acc3(lf_all, u)
    totals = jnp.sum(lf_all, axis=1, keepdims=True)
    later_pages = suf_ref[...]
    scores = []
    for ii in range(n_pp):
        sl = slice(ii * ROWS_PER_TOKEN, (ii + 1) * ROWS_PER_TOKEN)
        bias = jnp.concatenate([within[sl] + later_pages] * n_new, axis=0) + cnew_col
        kp = k_refs[ii][0, 0].reshape(H_FOX * HEAD_DIM, 128).astype(BF16)
        scores.append(_dot(qbd, kp) + bias)
        later_pages = later_pages + totals[sl]
    suf_ref[...] = later_pages

    def pv(p):
        out = None
        for ii in range(n_pp):
            vp = v_refs[ii][0, 0].reshape(H_FOX * HEAD_DIM, 128).astype(BF16)
            part = _dot_nt(p[:, ii * 128:(ii + 1) * 128], vp)
            out = part if out is None else out + part
        return out

    attend(jnp.concatenate(scores, axis=1), pv)

    @pl.when(g == pl.num_programs(1) - 1)
    def _():
        o_ref[0] = _collapse_heads(acc_ref[...] / l_ref[...], n_new, GROUP_W)


def _pages_per_step(n_pages):
    return min(32, n_pages)


def _fox_decode(layer, page_table, qbd, knew_t, vnew, lfcol, lfmat, cache_k, cache_v, cache_lf):
    nb, nrow, w = qbd.shape
    n_new = nrow // ROWS_PER_TOKEN
    n_pages = page_table.shape[1]
    n_pp = _pages_per_step(n_pages)
    u = (jnp.arange(128)[:, None] > jnp.arange(128)[None, :]).astype(BF16)
    per_seq = lambda a: pl.BlockSpec((1,) + a.shape[1:], lambda b, g, pt: (b,) + (0,) * (a.ndim - 1))

    def page_spec(ii):
        return pl.BlockSpec((1, 1) + cache_k.shape[2:],
                            lambda b, g, pt: (layer, pt[b, n_pages - 1 - (g * n_pp + ii)], 0, 0, 0))

    grid_spec = pltpu.PrefetchScalarGridSpec(
        num_scalar_prefetch=1,
        grid=(nb, n_pages // n_pp),
        in_specs=[per_seq(qbd), per_seq(knew_t), per_seq(vnew), per_seq(lfcol), per_seq(lfmat),
                  pl.BlockSpec((1,) + cache_lf.shape[1:], lambda b, g, pt: (layer, 0, 0, 0)),
                  pl.BlockSpec(u.shape, lambda b, g, pt: (0, 0))]
        + [page_spec(ii) for ii in range(n_pp)] * 2,
        out_specs=pl.BlockSpec((1, n_new, w), lambda b, g, pt: (b, 0, 0)),
        scratch_shapes=[pltpu.VMEM((nrow, 1), F32), pltpu.VMEM((nrow, 1), F32), pltpu.VMEM((nrow, w), F32),
                        pltpu.VMEM((ROWS_PER_TOKEN, 1), F32), pltpu.VMEM((n_pp * ROWS_PER_TOKEN, 128), F32)],
    )
    return pl.pallas_call(
        functools.partial(_fox_decode_kernel, n_pp=n_pp, n_new=n_new, n_pages=n_pages),
        grid_spec=grid_spec,
        out_shape=jax.ShapeDtypeStruct((nb, n_new, w), F32),
        compiler_params=_cparams("parallel", "arbitrary"),
        name="fox_decode",
    )(page_table, qbd, knew_t, vnew, lfcol, lfmat, cache_lf, u, *([cache_k] * n_pp), *([cache_v] * n_pp))


def _sb_decode_kernel(pt_ref, qbd_ref, knew_ref, vnew_ref, u_ref, *rest, n_pp, n_new):
    k_refs = rest[:n_pp]
    v_refs = rest[n_pp:2 * n_pp]
    o_ref = rest[2 * n_pp]
    rest_ref, acc_ref = rest[2 * n_pp + 1:]
    g = pl.program_id(1)
    nrow = n_new * ROWS_PER_TOKEN
    width = H_SB * HEAD_DIM
    qbd = qbd_ref[0]
    u = u_ref[...]

    def weights(z_blocks, valid):
        n = len(z_blocks)
        z = jnp.concatenate(z_blocks, axis=0)
        pr = jnp.maximum(z, jnp.log2(1.0 + jnp.exp2(jnp.minimum(z, EXP2_CLAMP))))
        if valid is not None:
            pr = jnp.where(valid, pr, 0.0)
        upto = _dot(pr.astype(BF16), u)
        run = rest_ref[...]
        out = []
        for ii in range(n):
            sl = slice(ii * nrow, (ii + 1) * nrow)
            a = jnp.exp2(z[sl] + upto[sl] + run)
            if valid is not None:
                a = jnp.where(valid, a, 0.0)
            out.append(a.astype(BF16))
            run = run + upto[sl, 0:1]
        rest_ref[...] = run
        return out

    @pl.when(g == 0)
    def _():
        rest_ref[...] = jnp.zeros_like(rest_ref)
        acc_ref[...] = jnp.zeros_like(acc_ref)
        lane = lax.broadcasted_iota(jnp.int32, (nrow, 128), 1)
        (a,) = weights([_dot(qbd, knew_ref[0])], lane < _row_token((nrow, 128)))
        acc_ref[...] = _dot(a, vnew_ref[0])

    z_blocks = [_dot(qbd, k_refs[ii][0, 0].reshape(width, 128).astype(BF16)) for ii in range(n_pp)]
    a_blocks = weights(z_blocks, None)
    out = acc_ref[...]
    for ii in range(n_pp):
        out = out + _dot_nt(a_blocks[ii], v_refs[ii][0, 0].reshape(width, 128).astype(BF16))
    acc_ref[...] = out

    @pl.when(g == pl.num_programs(1) - 1)
    def _():
        o_ref[0] = _collapse_heads(acc_ref[...], n_new, width)


def _sb_decode(layer, page_table, qbd, knew_t, vnew, cache_k, cache_v):
    nb, nrow, width = qbd.shape
    n_new = nrow // ROWS_PER_TOKEN
    n_pages = page_table.shape[1]
    n_pp = _pages_per_step(n_pages)
    u = -(jnp.arange(128)[:, None] >= jnp.arange(128)[None, :]).astype(BF16)
    per_seq = lambda a: pl.BlockSpec((1,) + a.shape[1:], lambda b, g, pt: (b,) + (0,) * (a.ndim - 1))

    def page_spec(ii):
        return pl.BlockSpec((1, 1) + cache_k.shape[2:],
                            lambda b, g, pt: (layer, pt[b, n_pages - 1 - (g * n_pp + ii)], 0, 0, 0))

    grid_spec = pltpu.PrefetchScalarGridSpec(
        num_scalar_prefetch=1,
        grid=(nb, n_pages // n_pp),
        in_specs=[per_seq(qbd), per_seq(knew_t), per_seq(vnew), pl.BlockSpec(u.shape, lambda b, g, pt: (0, 0))]
        + [page_spec(ii) for ii in range(n_pp)] * 2,
        out_specs=pl.BlockSpec((1, n_new, width), lambda b, g, pt: (b, 0, 0)),
        scratch_shapes=[pltpu.VMEM((nrow, 1), F32), pltpu.VMEM((nrow, width), F32)],
    )
    return pl.pallas_call(
        functools.partial(_sb_decode_kernel, n_pp=n_pp, n_new=n_new),
        grid_spec=grid_spec,
        out_shape=jax.ShapeDtypeStruct((nb, n_new, width), F32),
        compiler_params=_cparams("parallel", "arbitrary"),
        name="sb_decode",
    )(page_table, qbd, knew_t, vnew, u, *([cache_k] * n_pp), *([cache_v] * n_pp))


def _pad_cols(a, width=GROUP_W):
    return jnp.pad(a, ((0, 0), (0, width - a.shape[1])))


def _split_mix_weight(w):
    wf, wr, ws = H_FOX * HEAD_DIM, H_RET * HEAD_DIM, H_SB * HEAD_DIM
    sizes = (wf, wf, wf, H_FOX, wr, wr, wr, wr, ws, ws, ws)
    offs = np.cumsum((0,) + sizes)
    return [w[:, offs[i]:offs[i + 1]] for i in range(len(sizes))]


def _rope_swap_perm(n_heads):
    idx = np.arange(n_heads * HEAD_DIM)
    return (idx // HEAD_DIM) * HEAD_DIM + (idx % HEAD_DIM + HEAD_DIM // 2) % HEAD_DIM


def _rope_tables(pos, n_heads):
    half = HEAD_DIM // 2
    inv_freq = ROPE_BASE ** (-jnp.arange(half, dtype=F32) / half)
    ang = pos.astype(F32)[:, None] * inv_freq[None, :]
    cos, sin = jnp.cos(ang), jnp.sin(ang)
    cos_t = jnp.tile(jnp.concatenate([cos, cos], axis=1), (1, n_heads))
    sin_t = jnp.tile(jnp.concatenate([-sin, sin], axis=1), (1, n_heads))
    return _pad_cols(cos_t), _pad_cols(sin_t)


def _block_diag_queries(q, n_heads):
    nb, n_new, w = q.shape
    slot = jnp.arange(ROWS_PER_TOKEN)[:, None]
    lane_head = (jnp.arange(w) // HEAD_DIM)[None, :]
    mask = (slot == lane_head) & (slot < n_heads)
    qb = jnp.where(mask[None, None], q[:, :, None, :], 0.0)
    return qb.reshape(nb, n_new * ROWS_PER_TOKEN, w).astype(BF16)


def _new_keys_t(k, width):
    nb, n_new, _ = k.shape
    kt = jnp.swapaxes(k[:, :, :width], 1, 2)
    return jnp.pad(kt, ((0, 0), (0, 0), (0, 128 - n_new))).astype(BF16)


def _new_values(v, width):
    nb, n_new, _ = v.shape
    return jnp.pad(v[:, :, :width], ((0, 0), (0, 128 - n_new), (0, 0))).astype(BF16)


def kernel(x_prompt, x_sample, cache_fox_k, cache_fox_v, cache_fox_logf, cache_sb_k, cache_sb_v, state_ret,
           page_table, norm_g, w_ffn1_in, w_ffn1_out, w_mix_in, b_forget, ret_gn_g, w_mix_out, w_ffn2_in,
           w_ffn2_out):
    depth = norm_g.shape[0]
    bsz, seq, d = x_prompt.shape
    nb, n_new, _ = x_sample.shape
    n_pages, page_size = page_table.shape[1], cache_fox_k.shape[2]
    past = n_pages * page_size
    wf, wr, ws = H_FOX * HEAD_DIM, H_RET * HEAD_DIM, H_SB * HEAD_DIM
    w = GROUP_W

    ck_fox = jnp.transpose(cache_fox_k, (0, 1, 3, 4, 2))
    cv_fox = jnp.transpose(cache_fox_v, (0, 1, 3, 4, 2))
    ck_sb = jnp.transpose(cache_sb_k, (0, 1, 3, 4, 2))
    cv_sb = jnp.transpose(cache_sb_v, (0, 1, 3, 4, 2))
    clf = jnp.transpose(cache_fox_logf, (0, 3, 1, 2))

    cos_p, sin_p = _rope_tables(jnp.arange(seq, dtype=jnp.int32), H_RET)
    cos_s, sin_s = _rope_tables(past + jnp.arange(n_new, dtype=jnp.int32), H_RET)
    cos_s, sin_s = jnp.tile(cos_s, (nb, 1)), jnp.tile(sin_s, (nb, 1))
    perm = _rope_swap_perm(H_RET)
    avg = jnp.asarray(np.kron(np.eye(w // HEAD_DIM), np.full((HEAD_DIM, HEAD_DIM), 1.0 / HEAD_DIM)), BF16)

    yp = x_prompt.reshape(bsz * seq, d)
    ys = x_sample.reshape(nb * n_new, d)
    st_p, st_s = [], []
    for l in range(depth):
        g = [norm_g[l, i][None, :] for i in range(norm_g.shape[1])]
        w1_in, w1_out = w_ffn1_in[l].astype(BF16), w_ffn1_out[l].astype(BF16)
        w2_in, w2_out = w_ffn2_in[l].astype(BF16), w_ffn2_out[l].astype(BF16)
        qa, ka, va, fa, qr, kr, vr, gr, qs, ks, vs = _split_mix_weight(w_mix_in[l])
        ret_cols = [_pad_cols(qr), _pad_cols(qr[:, perm]), _pad_cols(kr), _pad_cols(kr[:, perm]),
                    _pad_cols(vr), _pad_cols(gr)]
        kv_cols = [ka, va, _pad_cols(ks), _pad_cols(vs)]
        wrow = jnp.concatenate([ka, _pad_cols(ks)] + ret_cols, axis=1).astype(BF16)
        wt = jnp.concatenate(kv_cols + [qa, _pad_cols(qs), _pad_cols(fa, 16)], axis=1).T.astype(BF16)
        w_samp = jnp.concatenate([qa, _pad_cols(qs)] + ret_cols + kv_cols + [_pad_cols(fa, 128)], axis=1).astype(BF16)
        bf_col = jnp.pad(b_forget[l], (0, 16 - H_FOX))[:, None]
        bf_row = jnp.pad(b_forget[l], (0, 128 - H_FOX))[None, :]
        gn = _pad_cols(ret_gn_g[l][None, :])
        wo = w_mix_out[l]
        wo1 = wo[:wf].astype(BF16)
        wo2 = jnp.pad(wo[wf:wf + wr], ((0, w - wr), (0, 0))).astype(BF16)
        wo3 = jnp.pad(wo[wf + wr:], ((0, w - ws), (0, 0))).astype(BF16)

        hp = _ffn_block(yp, g[0], w1_in, w1_out, g[1])
        (ksr, qr_p, kr_p, vr_p, gr_p, kaug, kat, vat, kst, vst, vstb, qat, qst, lft, qaug, vtaug) = _proj_prompt(
            hp.reshape(bsz, seq, d), g[2], wrow, wt, bf_col, cos_p, sin_p)
        o_fox = _fox_prompt(kaug, qat, qaug, vtaug)
        o_ret, s_raw = _ret_prompt(qr_p, kr_p, vr_p)
        o_sb_a, o_sb_b = _sb_prompt(ksr, qst, vstb)
        yp = _merge_ffn(hp, o_fox.reshape(bsz * seq, w), o_ret.reshape(bsz * seq, w), gr_p.reshape(bsz * seq, w),
                        o_sb_a.reshape(bsz * seq, -1), o_sb_b.reshape(bsz * seq, -1), gn, avg, wo1, wo2, wo3, g[3],
                        g[4], w2_in, w2_out, g[5])
        s_fin = jnp.stack([s_raw[:, h // 2, h % 2, (h % 2) * HEAD_DIM:(h % 2 + 1) * HEAD_DIM,
                                 (h % 2) * HEAD_DIM:(h % 2 + 1) * HEAD_DIM] for h in range(H_RET)], axis=1)
        to_heads = lambda a, nh: jnp.transpose(a.reshape(bsz, nh, HEAD_DIM, seq), (0, 3, 1, 2))
        st_p.append((to_heads(kat, H_FOX), to_heads(vat, H_FOX), jnp.transpose(lft[:, :H_FOX], (0, 2, 1)),
                     to_heads(kst, H_SB), to_heads(vst, H_SB), s_fin))

        hs = _ffn_block(ys, g[0], w1_in, w1_out, g[1])
        (qa_s, qs_s, qr_s, kr_s, vr_s, gr_s, ka_s, va_s, ks_s, vs_s, lf_s) = _proj_sample(
            hs, g[2], w_samp, bf_row, cos_s, sin_s)
        seqs = lambda a: a.reshape(nb, n_new, a.shape[-1])
        lf_new = seqs(lf_s)[:, :, :ROWS_PER_TOKEN]
        lfcol = lf_new.reshape(nb, n_new * ROWS_PER_TOKEN, 1)
        lfmat = jnp.broadcast_to(jnp.transpose(lf_new, (0, 2, 1))[:, None], (nb, n_new, ROWS_PER_TOKEN, n_new))
        lfmat = jnp.pad(lfmat.reshape(nb, n_new * ROWS_PER_TOKEN, n_new), ((0, 0), (0, 0), (0, 128 - n_new)))
        o_fox_s = _fox_decode(l, page_table, _block_diag_queries(seqs(qa_s), H_FOX), _new_keys_t(seqs(ka_s), wf),
                              _new_values(seqs(va_s), wf), lfcol, lfmat, ck_fox, cv_fox, clf)
        o_sb_s = _sb_decode(l, page_table, _block_diag_queries(seqs(qs_s)[:, :, :ws] * LOG2E, H_SB),
                            _new_keys_t(seqs(ks_s), ws), _new_values(seqs(vs_s), ws), ck_sb, cv_sb)
        o_ret_s, s_new = _ret_sample(seqs(qr_s), seqs(kr_s), seqs(vr_s), state_ret[l])
        o_sb_rows = _pad_cols(o_sb_s.reshape(nb * n_new, ws)).astype(BF16)
        ys = _merge_ffn(hs, o_fox_s.reshape(nb * n_new, w).astype(BF16), o_ret_s.reshape(nb * n_new, w), gr_s,
                        o_sb_rows[:, :o_sb_a.shape[-1]], o_sb_rows[:, o_sb_a.shape[-1]:], gn, avg, wo1, wo2, wo3, g[3],
                        g[4], w2_in, w2_out, g[5])
        heads = lambda a, nh: a[:, :nh * HEAD_DIM].reshape(nb, n_new, nh, HEAD_DIM)
        st_s.append((heads(ka_s, H_FOX), heads(va_s, H_FOX), seqs(lf_s)[:, :, :H_FOX],
                     heads(ks_s, H_SB), heads(vs_s, H_SB), s_new))

    stk = lambda sts, i: jnp.stack([s[i] for s in sts], axis=0)
    return (yp.reshape(bsz, seq, d), ys.reshape(nb, n_new, d),
            stk(st_p, 0), stk(st_p, 1), stk(st_p, 2), stk(st_p, 3), stk(st_p, 4), stk(st_p, 5),
            stk(st_s, 0), stk(st_s, 1), stk(st_s, 2), stk(st_s, 3), stk(st_s, 4), stk(st_s, 5))
```

```python
import functools
import math

import numpy as np
import jax
import jax.numpy as jnp
from jax import lax
from jax.experimental import pallas as pl
from jax.experimental.pallas import tpu as pltpu

HEAD_DIM = 64
H_FOX = 6
H_RET = 5
H_SB = 5
GROUP_W = 384
HEADS_PER_BLOCK = 2
EPS = 1e-6
ROPE_BASE = 10000.0
Q_SCALE = HEAD_DIM ** -0.5
LOG2E = math.log2(math.e)
AUG_ROWS = 16
V_ROWS = 80
CUM_BLOCK = 256
EXP2_CLAMP = 64.0
FF_CHUNK = 256
RET_CHUNK = 512
PROMPT_TQ = 1024
FOX_TK = 1024
SB_TK = 1024
VMEM_LIMIT_BYTES = 48 * 1024 * 1024
F32 = jnp.float32
BF16 = jnp.bfloat16
NEG_INF = float("-inf")


def _cparams(*sem):
    return pltpu.CompilerParams(dimension_semantics=sem, vmem_limit_bytes=VMEM_LIMIT_BYTES)


def _dot(a, b):
    return jnp.dot(a, b, preferred_element_type=F32)


def _dot_nt(a, b):
    return lax.dot_general(a, b, (((1,), (1,)), ((), ())), preferred_element_type=F32)


def _dot_tn(a, b):
    return lax.dot_general(a, b, (((0,), (0,)), ((), ())), preferred_element_type=F32)


def _split3(x):
    hi = x.astype(BF16)
    r = x - hi.astype(F32)
    mid = r.astype(BF16)
    lo = (r - mid.astype(F32)).astype(BF16)
    return hi, mid, lo


def _dot_acc3(x, m):
    hi, mid, lo = _split3(x)
    return _dot(hi, m) + _dot(mid, m) + _dot(lo, m)


def _rms(x, g):
    return x * lax.rsqrt(jnp.mean(x * x, axis=-1, keepdims=True) + EPS) * g


def _softplus_parts(z):
    lp = jnp.log1p(jnp.exp(-jnp.abs(z)))
    return -jnp.maximum(z, 0.0) - lp, jnp.minimum(z, 0.0) - lp


def _ffn_rows(x, gpre_ref, wg_ref, wu_ref, wo_ref, gpost_ref):
    xn = _rms(x, gpre_ref[...]).astype(BF16)
    hidden = []
    for c in range(wg_ref.shape[1] // FF_CHUNK):
        sl = slice(c * FF_CHUNK, (c + 1) * FF_CHUNK)
        gate = _dot(xn, wg_ref[:, sl])
        up = _dot(xn, wu_ref[:, sl])
        hidden.append((gate * jax.nn.sigmoid(gate) * up).astype(BF16))
    y = _dot(jnp.concatenate(hidden, axis=1), wo_ref[...])
    return x + 0.5 * _rms(y, gpost_ref[...])


def _ffn_kernel(x_ref, gpre_ref, wg_ref, wu_ref, wo_ref, gpost_ref, o_ref):
    o_ref[...] = _ffn_rows(x_ref[...], gpre_ref, wg_ref, wu_ref, wo_ref, gpost_ref)


def _ffn_block(x, g_pre, w_in, w_out, g_post, layer):
    m, d = x.shape
    dff = w_out.shape[1]
    tm = min(512, m)
    once = pl.Buffered(1)
    return pl.pallas_call(
        _ffn_kernel,
        grid=(m // tm,),
        in_specs=[
            pl.BlockSpec((tm, d), lambda i: (i, 0)),
            pl.BlockSpec((1, d), lambda i: (0, 0)),
            pl.BlockSpec((None, d, dff), lambda i: (layer, 0, 0), pipeline_mode=once),
            pl.BlockSpec((None, d, dff), lambda i: (layer, 0, 1), pipeline_mode=once),
            pl.BlockSpec((None, dff, d), lambda i: (layer, 0, 0), pipeline_mode=once),
            pl.BlockSpec((1, d), lambda i: (0, 0)),
        ],
        out_specs=pl.BlockSpec((tm, d), lambda i: (i, 0)),
        out_shape=jax.ShapeDtypeStruct((m, d), F32),
        compiler_params=_cparams("parallel"),
        name="ffn_block",
    )(x, g_pre, w_in, w_in, w_out, g_post)


def _proj_prompt_kernel(*refs, n_alias):
    x_ref, g_ref, wrow_ref, wt_ref, bf_ref, cos_ref, sin_ref, tri_ref, kplace_ref, qplace_ref = refs[:10]
    (ksr_ref, qr_ref, kr_ref, vr_ref, gr_ref, kaug_ref, kat_ref, vat_ref, kst_ref, vst_ref,
     vstb_ref, qat_ref, qst_ref, lf_ref, qaug_ref, vtaug_ref, carry_ref) = refs[10 + n_alias:]
    w = GROUP_W
    t = pl.program_id(1)

    @pl.when(t == 0)
    def _():
        carry_ref[...] = jnp.zeros_like(carry_ref)

    xn = _rms(x_ref[0], g_ref[...]).astype(BF16)
    pr = _dot(xn, wrow_ref[...])
    cos = cos_ref[...]
    sin = sin_ref[...]
    ksr_ref[0] = pr[:, w:2 * w].astype(BF16)
    qr_ref[0] = (pr[:, 2 * w:3 * w] * cos + pr[:, 3 * w:4 * w] * sin).astype(BF16)
    kr_ref[0] = ((pr[:, 4 * w:5 * w] * cos + pr[:, 5 * w:6 * w] * sin) * Q_SCALE).astype(BF16)
    vr_ref[0] = pr[:, 6 * w:7 * w].astype(BF16)
    gr_ref[0] = pr[:, 7 * w:8 * w]

    pt = _dot_nt(wt_ref[...], xn)
    kat = pt[0:w]
    vat = pt[w:2 * w]
    kst = pt[2 * w:3 * w]
    vst = pt[3 * w:4 * w]
    kat_ref[0] = kat
    vat_ref[0] = vat
    kst_ref[0] = kst[0:H_SB * HEAD_DIM]
    vst_ref[0] = vst[0:H_SB * HEAD_DIM]
    vstb_ref[0] = vst.astype(BF16)
    qat_ref[0] = (pt[4 * w:5 * w] * (Q_SCALE * LOG2E)).astype(BF16)
    qst_ref[0] = (pt[5 * w:6 * w] * (Q_SCALE * LOG2E)).astype(BF16)

    logf = jax.nn.log_sigmoid(pt[6 * w:6 * w + 16] + bf_ref[...])
    lf_ref[0] = logf
    c = _dot_acc3(logf, tri_ref[...]) + carry_ref[...]
    carry_ref[...] = c[:, -1:]

    tm = c.shape[1]
    one_row = jnp.concatenate([jnp.ones((1, tm), BF16), jnp.zeros((15, tm), BF16)], axis=0)
    csplit = jnp.concatenate(list(_split3(c * LOG2E)) + [one_row], axis=0)
    for p in range(w // 128):
        kaug_ref[0, p, :, 0:128] = pr[:, p * 128:(p + 1) * 128].astype(BF16)
        kaug_ref[0, p, :, 128:256] = _dot_tn(csplit, kplace_ref[p]).astype(BF16)
    q_rows = _dot(qplace_ref[...], csplit).astype(BF16)
    for h in range(H_FOX):
        qaug_ref[0, h] = q_rows[h * AUG_ROWS:(h + 1) * AUG_ROWS]
        vtaug_ref[0, h, 0:HEAD_DIM, :] = vat[h * HEAD_DIM:(h + 1) * HEAD_DIM].astype(BF16)
        vtaug_ref[0, h, HEAD_DIM:V_ROWS, :] = one_row


def _fox_placements():
    nb = GROUP_W // 128
    kplace = np.zeros((nb, 64, 128), np.float32)
    qplace = np.zeros((H_FOX * AUG_ROWS, 64), np.float32)
    for h in range(H_FOX):
        p, hh = divmod(h, HEADS_PER_BLOCK)
        for sp in range(3):
            kplace[p, sp * 16 + h, hh * 3 + sp] = -1.0
            kplace[p, 48, HEADS_PER_BLOCK * 3 + sp] = 1.0
            qplace[h * AUG_ROWS + hh * 3 + sp, 48] = 1.0
            qplace[h * AUG_ROWS + HEADS_PER_BLOCK * 3 + sp, sp * 16 + h] = 1.0
    return jnp.asarray(kplace, BF16), jnp.asarray(qplace, BF16)


def _proj_prompt(x, g, wrow, wt, bf, cos, sin, layer, depth, stacked):
    b, t, d = x.shape
    tm = min(512, t)
    w = GROUP_W
    nb = w // 128
    tri = (jnp.arange(tm)[:, None] <= jnp.arange(tm)[None, :]).astype(BF16)
    kplace, qplace = _fox_placements()
    row = lambda dt: jax.ShapeDtypeStruct((b, t, w), dt)
    col = lambda n, dt: jax.ShapeDtypeStruct((b, n, t), dt)
    rspec = pl.BlockSpec((1, tm, w), lambda bi, ti: (bi, ti, 0))
    cspec = lambda n: pl.BlockSpec((1, n, tm), lambda bi, ti: (bi, 0, ti))
    lspec = lambda n: pl.BlockSpec((None, 1, n, tm), lambda bi, ti: (layer, bi, 0, ti))
    lcol = lambda n: jax.ShapeDtypeStruct((depth, b, n, t), F32)
    const = lambda a: pl.BlockSpec(a.shape, lambda bi, ti: (0,) * a.ndim)
    nsb = H_SB * HEAD_DIM
    stacked = tuple(stacked or ())
    return pl.pallas_call(
        functools.partial(_proj_prompt_kernel, n_alias=len(stacked)),
        grid=(b, t // tm),
        in_specs=[
            pl.BlockSpec((1, tm, d), lambda bi, ti: (bi, ti, 0)),
            const(g), const(wrow), const(wt), const(bf),
            pl.BlockSpec((tm, w), lambda bi, ti: (ti, 0)),
            pl.BlockSpec((tm, w), lambda bi, ti: (ti, 0)),
            const(tri), const(kplace), const(qplace),
        ] + [pl.BlockSpec(memory_space=pl.ANY)] * len(stacked),
        input_output_aliases={10 + k: 6 + k for k in range(len(stacked))},
        out_specs=[rspec] * 5 + [pl.BlockSpec((1, nb, tm, 256), lambda bi, ti: (bi, 0, ti, 0))]
        + [lspec(w), lspec(w), lspec(nsb), lspec(nsb)] + [cspec(w)] * 3 + [cspec(16)]
        + [pl.BlockSpec((1, H_FOX, AUG_ROWS, tm), lambda bi, ti: (bi, 0, 0, ti)),
           pl.BlockSpec((1, H_FOX, V_ROWS, tm), lambda bi, ti: (bi, 0, 0, ti))],
        out_shape=[row(BF16)] * 4 + [row(F32), jax.ShapeDtypeStruct((b, nb, t, 256), BF16)]
        + [lcol(w), lcol(w), lcol(nsb), lcol(nsb)] + [col(w, BF16)] * 3 + [col(16, F32)]
        + [jax.ShapeDtypeStruct((b, H_FOX, AUG_ROWS, t), BF16), jax.ShapeDtypeStruct((b, H_FOX, V_ROWS, t), BF16)],
        scratch_shapes=[pltpu.VMEM((16, 1), F32)],
        compiler_params=_cparams("parallel", "arbitrary"),
        name="proj_prompt",
    )(x, g, wrow, wt, bf, cos, sin, tri, kplace, qplace, *stacked)


def _proj_sample_kernel(x_ref, g_ref, w_ref, bf_ref, cos_ref, sin_ref,
                        qa_ref, qs_ref, qr_ref, kr_ref, vr_ref, gr_ref,
                        ka_ref, va_ref, ks_ref, vs_ref, lf_ref):
    w = GROUP_W
    xn = _rms(x_ref[...], g_ref[...]).astype(BF16)
    pr = _dot(xn, w_ref[...])
    cos = cos_ref[...]
    sin = sin_ref[...]
    qa_ref[...] = pr[:, 0:w] * Q_SCALE
    qs_ref[...] = pr[:, w:2 * w] * Q_SCALE
    qr_ref[...] = pr[:, 2 * w:3 * w] * cos + pr[:, 3 * w:4 * w] * sin
    kr_ref[...] = (pr[:, 4 * w:5 * w] * cos + pr[:, 5 * w:6 * w] * sin) * Q_SCALE
    vr_ref[...] = pr[:, 6 * w:7 * w]
    gr_ref[...] = pr[:, 7 * w:8 * w]
    ka_ref[...] = pr[:, 8 * w:9 * w]
    va_ref[...] = pr[:, 9 * w:10 * w]
    ks_ref[...] = pr[:, 10 * w:11 * w]
    vs_ref[...] = pr[:, 11 * w:12 * w]
    lf_ref[...] = jax.nn.log_sigmoid(pr[:, 12 * w:12 * w + 128] + bf_ref[...])


def _proj_sample(x, g, w_all, bf_row, cos, sin):
    m = x.shape[0]
    w = GROUP_W
    return pl.pallas_call(
        _proj_sample_kernel,
        out_shape=[jax.ShapeDtypeStruct((m, w), F32)] * 10 + [jax.ShapeDtypeStruct((m, 128), F32)],
        compiler_params=pltpu.CompilerParams(vmem_limit_bytes=VMEM_LIMIT_BYTES),
        name="proj_sample",
    )(x, g, w_all, bf_row, cos, sin)


def _merge_ffn_kernel(h_ref, of_ref, oret_ref, gr_ref, osba_ref, osbb_ref, gn_ref, avg_ref,
                      w1_ref, w2_ref, w3a_ref, w3b_ref, g_ref,
                      gpre_ref, wg_ref, wu_ref, wo_ref, gpost_ref, o_ref):
    x = oret_ref[...]
    avg = avg_ref[...]
    hi, mid, lo = _split3(x)
    mu = _dot(hi, avg) + _dot(mid, avg) + _dot(lo, avg)
    dev = x - mu
    var = _dot_acc3(dev * dev, avg)
    gate = gr_ref[...]
    r = dev * lax.rsqrt(var + EPS) * gn_ref[...] * (gate * jax.nn.sigmoid(gate))
    y = (_dot(of_ref[...], w1_ref[...]) + _dot(r.astype(BF16), w2_ref[...])
         + _dot(osba_ref[...], w3a_ref[...]) + _dot(osbb_ref[...], w3b_ref[...]))
    mixed = h_ref[...] + _rms(y, g_ref[...])
    o_ref[...] = _ffn_rows(mixed, gpre_ref, wg_ref, wu_ref, wo_ref, gpost_ref)


def _merge_ffn(h, o_fox, o_ret, gr, o_sb_a, o_sb_b, gn, avg, w1, w2, w3, g, g_pre, w_in, w_out, g_post, layer):
    m, d = h.shape
    w = GROUP_W
    dff = w_out.shape[1]
    wa = o_sb_a.shape[1]
    w3a, w3b = w3[:wa], w3[wa:]
    tm = min(512, m)
    once = pl.Buffered(1)
    rows = lambda n: pl.BlockSpec((tm, n), lambda i: (i, 0))
    const = lambda a: pl.BlockSpec(a.shape, lambda i: (0,) * a.ndim)
    return pl.pallas_call(
        _merge_ffn_kernel,
        grid=(m // tm,),
        in_specs=[rows(d), rows(w), rows(w), rows(w), rows(wa), rows(w - wa), const(gn), const(avg),
                  const(w1), const(w2), const(w3a), const(w3b), const(g), const(g_pre),
                  pl.BlockSpec((None, d, dff), lambda i: (layer, 0, 0), pipeline_mode=once),
                  pl.BlockSpec((None, d, dff), lambda i: (layer, 0, 1), pipeline_mode=once),
                  pl.BlockSpec((None, dff, d), lambda i: (layer, 0, 0), pipeline_mode=once),
                  const(g_post)],
        out_specs=rows(d),
        out_shape=jax.ShapeDtypeStruct((m, d), F32),
        compiler_params=_cparams("parallel"),
        name="merge_ffn",
    )(h, o_fox, o_ret, gr, o_sb_a, o_sb_b, gn, avg, w1, w2, w3a, w3b, g, g_pre, w_in, w_in, w_out, g_post)


def _head_lane_mask(shape, hh):
    lane = lax.broadcasted_iota(jnp.int32, shape, len(shape) - 1)
    return (lane >= HEAD_DIM * hh) & (lane < HEAD_DIM * (hh + 1))


def _head_sublane_mask(shape, hh):
    r = lax.broadcasted_iota(jnp.int32, shape, 0)
    return (r >= HEAD_DIM * hh) & (r < HEAD_DIM * (hh + 1))


def _fox_prompt_kernel(kaug_ref, qt_ref, qaug_ref, vt_ref, o_ref, qa_ref, *, tq, tk):
    i = pl.program_id(2)
    qt2 = qt_ref[0]
    kdim = kaug_ref.shape[-1]
    zpad = jnp.zeros((kdim - 128 - AUG_ROWS, tq), BF16)
    for hh in range(HEADS_PER_BLOCK):
        qa_ref[hh] = jnp.concatenate(
            [jnp.where(_head_sublane_mask(qt2.shape, hh), qt2, jnp.zeros_like(qt2)), qaug_ref[0, hh], zpad], axis=0)
    krow = lax.broadcasted_iota(jnp.int32, (tk, tq), 0)
    qcol = lax.broadcasted_iota(jnp.int32, (tk, tq), 1)
    sub = tq // tk

    def step(j, carry, diag):
        off = pl.multiple_of(j * tk, tk)
        kt = kaug_ref[0, 0, pl.ds(off, tk), :]
        new = []
        for hh in range(HEADS_PER_BLOCK):
            m, acc = carry[hh]
            s = _dot(kt, qa_ref[hh])
            if diag is not None:
                s = jnp.where(krow + diag * tk <= qcol, s, NEG_INF)
            m_new = jnp.maximum(m, jnp.max(s, axis=0, keepdims=True))
            alpha = jnp.exp2(m - m_new)
            p = jnp.exp2(s - m_new).astype(BF16)
            vt = vt_ref[0, hh, :, pl.ds(off, tk)]
            new.append((m_new, alpha * acc + _dot(vt, p)))
        return tuple(new)

    init = tuple((jnp.full((1, tq), NEG_INF, F32), jnp.zeros((V_ROWS, tq), F32)) for _ in range(HEADS_PER_BLOCK))
    carry = lax.fori_loop(0, i * sub, functools.partial(step, diag=None), init)
    for d in range(sub):
        carry = step(i * sub + d, carry, d)
    o_t = jnp.concatenate([acc[0:HEAD_DIM] / acc[HEAD_DIM:HEAD_DIM + 1] for _, acc in carry], axis=0)
    o_ref[0] = o_t.T.astype(o_ref.dtype)


def _fox_prompt(kaug, qt, qaug, vtaug):
    b, nb, t, kdim = kaug.shape
    tq = min(PROMPT_TQ, t)
    tk = min(FOX_TK, tq)
    return pl.pallas_call(
        functools.partial(_fox_prompt_kernel, tq=tq, tk=tk),
        grid=(b, nb, t // tq),
        in_specs=[
            pl.BlockSpec((1, 1, t, kdim), lambda bi, p, i: (bi, p, 0, 0)),
            pl.BlockSpec((1, 128, tq), lambda bi, p, i: (bi, p, i)),
            pl.BlockSpec((1, HEADS_PER_BLOCK, AUG_ROWS, tq), lambda bi, p, i: (bi, p, 0, i)),
            pl.BlockSpec((1, HEADS_PER_BLOCK, V_ROWS, t), lambda bi, p, i: (bi, p, 0, 0)),
        ],
        out_specs=pl.BlockSpec((1, tq, 128), lambda bi, p, i: (bi, i, p)),
        out_shape=jax.ShapeDtypeStruct((b, t, nb * 128), BF16),
        scratch_shapes=[pltpu.VMEM((HEADS_PER_BLOCK, kdim, tq), BF16)],
        compiler_params=_cparams("parallel", "parallel", "arbitrary"),
        name="fox_prompt",
    )(kaug, qt, qaug, vtaug)


def _sb_prompt_kernel(k_ref, qt_ref, vt_ref, ut_ref, o_ref, q_ref, *, tq, tk, chains):
    i = pl.program_id(2)
    for c, (bb, hh) in enumerate(chains):
        qt2 = qt_ref[bb]
        q_ref[c] = jnp.where(_head_sublane_mask(qt2.shape, hh), qt2, jnp.zeros_like(qt2))
    ut = ut_ref[...]
    cb = ut.shape[0]
    nsub = tk // cb
    krow = lax.broadcasted_iota(jnp.int32, (tk, tq), 0)
    qcol = lax.broadcasted_iota(jnp.int32, (tk, tq), 1)
    sub = tq // tk

    def step(j, carry, diag):
        off = pl.multiple_of(j * tk, tk)
        new = []
        for c, (bb, hh) in enumerate(chains):
            rest, acc = carry[c]
            z = _dot(k_ref[bb, pl.ds(off, tk), :], q_ref[c])
            pr = jnp.maximum(z, jnp.log2(1.0 + jnp.exp2(jnp.minimum(z, EXP2_CLAMP))))
            if diag is not None:
                valid = krow + diag * tk < qcol
                pr = jnp.where(valid, pr, 0.0)
            blocks = [None] * nsub
            for blk in reversed(range(nsub)):
                sl = slice(blk * cb, (blk + 1) * cb)
                upto = _dot(ut, pr[sl].astype(BF16))
                blocks[blk] = jnp.exp2(z[sl] + upto + rest)
                rest = rest + upto[0:1]
            a = blocks[0] if nsub == 1 else jnp.concatenate(blocks, axis=0)
            if diag is not None:
                a = jnp.where(valid, a, 0.0)
            vt = vt_ref[bb, HEAD_DIM * hh:HEAD_DIM * (hh + 1), pl.ds(off, tk)]
            new.append((rest, acc + _dot(vt, a.astype(BF16))))
        return tuple(new)

    carry = tuple((jnp.zeros((1, tq), F32), jnp.zeros((HEAD_DIM, tq), F32)) for _ in chains)
    for d in reversed(range(sub)):
        carry = step(i * sub + d, carry, d)
    final = lax.fori_loop(0, i * sub, lambda n, c: step(i * sub - 1 - n, c, None), carry)
    for bb in range(o_ref.shape[0]):
        rows = [jnp.zeros((HEAD_DIM, tq), F32)] * HEADS_PER_BLOCK
        for c, (cb_, hh) in enumerate(chains):
            if cb_ == bb:
                rows[hh] = final[c][1]
        o_ref[bb] = jnp.concatenate(rows, axis=0).T.astype(o_ref.dtype)


def _sb_prompt_call(krows, qt, vt, *, first_pair, n_pairs, batch_block, chains):
    b, t, _ = krows.shape
    tq = min(PROMPT_TQ, t)
    tk = min(SB_TK, tq)
    cb = min(CUM_BLOCK, tk)
    ut = -(jnp.arange(cb)[None, :] >= jnp.arange(cb)[:, None]).astype(BF16)
    nbb = batch_block
    return pl.pallas_call(
        functools.partial(_sb_prompt_kernel, tq=tq, tk=tk, chains=chains),
        grid=(b // nbb, n_pairs, t // tq),
        in_specs=[
            pl.BlockSpec((nbb, t, 128), lambda bi, p, i: (bi, 0, first_pair + p)),
            pl.BlockSpec((nbb, 128, tq), lambda bi, p, i: (bi, first_pair + p, i)),
            pl.BlockSpec((nbb, 128, t), lambda bi, p, i: (bi, first_pair + p, 0)),
            pl.BlockSpec((cb, cb), lambda bi, p, i: (0, 0)),
        ],
        out_specs=pl.BlockSpec((nbb, tq, 128), lambda bi, p, i: (bi, i, p)),
        out_shape=jax.ShapeDtypeStruct((b, t, n_pairs * 128), BF16),
        scratch_shapes=[pltpu.VMEM((len(chains), 128, tq), BF16)],
        compiler_params=_cparams("parallel", "parallel", "arbitrary"),
        name="sb_prompt",
    )(krows, qt, vt, ut)


def _sb_prompt(krows, qt, vt):
    b = krows.shape[0]
    full, left = divmod(H_SB, HEADS_PER_BLOCK)
    outs = [_sb_prompt_call(krows, qt, vt, first_pair=0, n_pairs=full, batch_block=1,
                            chains=tuple((0, hh) for hh in range(HEADS_PER_BLOCK)))]
    if left:
        nbb = 2 if b % 2 == 0 else 1
        outs.append(_sb_prompt_call(krows, qt, vt, first_pair=full, n_pairs=1, batch_block=nbb,
                                    chains=tuple((bb, 0) for bb in range(nbb))))
    return outs


def _log_gamma(head):
    pw = jnp.zeros(head.shape, F32)
    for h in range(HEADS_PER_BLOCK * (GROUP_W // 128)):
        pw = jnp.where(head == h, 2.0 ** (-5.0 - h), pw)
    return jnp.log(1.0 - pw)


def _ret_prompt_kernel(q_ref, k_ref, v_ref, o_ref, s_ref, state_ref, dec_ref, qkd_ref, *, ch):
    p = pl.program_id(1)
    c = pl.program_id(2)

    @pl.when(c == 0)
    def _():
        state_ref[...] = jnp.zeros_like(state_ref)
        ii = lax.broadcasted_iota(jnp.int32, (ch, ch), 0)
        jj = lax.broadcasted_iota(jnp.int32, (ch, ch), 1)
        diff = ii - jj
        pos = lax.broadcasted_iota(jnp.int32, (ch, 1), 0).astype(F32)
        for hh in range(HEADS_PER_BLOCK):
            lg = _log_gamma(jnp.full((1, 1), HEADS_PER_BLOCK * p + hh, jnp.int32))
            dec_ref[hh] = jnp.where(diff >= 0, jnp.exp(jnp.maximum(diff, 0).astype(F32) * lg), 0.0)
            qkd_ref[hh, :, 0:1] = jnp.exp((pos + 1.0) * lg)
            qkd_ref[hh, :, 1:2] = jnp.exp((ch - 1.0 - pos) * lg)
            qkd_ref[hh, :, 2:3] = jnp.broadcast_to(jnp.exp(ch * lg), (ch, 1))

    q2 = q_ref[0]
    k2 = k_ref[0]
    v2 = v_ref[0]
    outs = []
    for hh in range(HEADS_PER_BLOCK):
        qh = jnp.where(_head_lane_mask(q2.shape, hh), q2, jnp.zeros_like(q2))
        scores = _dot_nt(qh, k2) * dec_ref[hh]
        inner = _dot(scores.astype(BF16), v2)
        state = state_ref[hh]
        q_dec = (qh.astype(F32) * qkd_ref[hh, :, 0:1]).astype(BF16)
        cross = _dot(q_dec, state.astype(BF16))
        outs.append(inner + cross)
        k_dec = (k2.astype(F32) * qkd_ref[hh, :, 1:2]).astype(BF16)
        state_ref[hh] = qkd_ref[hh, 0:1, 2:3] * state + _dot_tn(k_dec, v2)
    o_ref[0] = jnp.where(_head_lane_mask(outs[0].shape, 0), outs[0], outs[1])

    @pl.when(c == pl.num_programs(2) - 1)
    def _():
        s_ref[0, 0] = state_ref[...]


def _ret_prompt(q, k, v):
    b, t, w = q.shape
    ch = min(RET_CHUNK, t)
    nb = w // 128
    tile = pl.BlockSpec((1, ch, 128), lambda bi, p, c: (bi, c, p))
    return pl.pallas_call(
        functools.partial(_ret_prompt_kernel, ch=ch),
        grid=(b, nb, t // ch),
        in_specs=[tile, tile, tile],
        out_specs=[tile, pl.BlockSpec((1, 1, HEADS_PER_BLOCK, 128, 128), lambda bi, p, c: (bi, p, 0, 0, 0))],
        out_shape=[jax.ShapeDtypeStruct((b, t, w), F32),
                   jax.ShapeDtypeStruct((b, nb, HEADS_PER_BLOCK, 128, 128), F32)],
        scratch_shapes=[pltpu.VMEM((HEADS_PER_BLOCK, 128, 128), F32),
                        pltpu.VMEM((HEADS_PER_BLOCK, ch, ch), F32),
                        pltpu.VMEM((HEADS_PER_BLOCK, ch, 3), F32)],
        compiler_params=_cparams("parallel", "parallel", "arbitrary"),
        name="ret_prompt",
    )(q, k, v)


def _ret_sample_kernel(q_ref, k_ref, v_ref, s_ref, o_ref, snew_ref, *, n_new):
    q = q_ref[0]
    k = k_ref[0]
    v = v_ref[0]
    ii = lax.broadcasted_iota(jnp.int32, (n_new, n_new), 0)
    jj = lax.broadcasted_iota(jnp.int32, (n_new, n_new), 1)
    diff = ii - jj
    pos = lax.broadcasted_iota(jnp.int32, (n_new, 1), 0).astype(F32)
    o_ref[0] = jnp.zeros(o_ref.shape[1:], F32)
    for h in range(H_RET):
        lg = _log_gamma(jnp.full((1, 1), h, jnp.int32))
        sl = slice(h * HEAD_DIM, (h + 1) * HEAD_DIM)
        qh = q[:, sl]
        kh = k[:, sl]
        vh = v[:, sl].astype(BF16)
        decay = jnp.where(diff >= 0, jnp.exp(jnp.maximum(diff, 0).astype(F32) * lg), 0.0)
        scores = _dot_nt(qh.astype(BF16), kh.astype(BF16)) * decay
        inner = _dot(scores.astype(BF16), vh)
        state = s_ref[0, h]
        cross = _dot((qh * jnp.exp((pos + 1.0) * lg)).astype(BF16), state.astype(BF16))
        o_ref[0, :, sl] = inner + cross
        k_dec = (kh * jnp.exp((n_new - 1.0 - pos) * lg)).astype(BF16)
        snew_ref[0, h] = jnp.exp(n_new * lg) * state + _dot_tn(k_dec, vh)


def _ret_sample(q, k, v, state):
    nb, n_new, w = q.shape
    tok = pl.BlockSpec((1, n_new, w), lambda i: (i, 0, 0))
    st = pl.BlockSpec((1, H_RET, HEAD_DIM, HEAD_DIM), lambda i: (i, 0, 0, 0))
    return pl.pallas_call(
        functools.partial(_ret_sample_kernel, n_new=n_new),
        grid=(nb,),
        in_specs=[tok, tok, tok, st],
        out_specs=[tok, st],
        out_shape=[jax.ShapeDtypeStruct((nb, n_new, w), F32),
                   jax.ShapeDtypeStruct(state.shape, F32)],
        compiler_params=_cparams("parallel"),
        name="ret_sample",
    )(q, k, v, state)


ROWS_PER_TOKEN = 8


def _row_token(shape):
    return lax.broadcasted_iota(jnp.int32, shape, 0) // ROWS_PER_TOKEN


def _collapse_heads(acc, n_new, width):
    r = lax.broadcasted_iota(jnp.int32, acc.shape, 0) % ROWS_PER_TOKEN
    lane_head = lax.broadcasted_iota(jnp.int32, acc.shape, 1) // HEAD_DIM
    kept = jnp.where(r == lane_head, acc, 0.0)
    return jnp.sum(kept.reshape(n_new, ROWS_PER_TOKEN, width), axis=1)


def _fox_decode_kernel(pt_ref, qbd_ref, knew_ref, vnew_ref, lfcol_ref, lfmat_ref, lfc_ref, u_ref, *rest,
                       n_pp, n_new, n_pages):
    k_refs = rest[:n_pp]
    v_refs = rest[n_pp:2 * n_pp]
    o_ref = rest[2 * n_pp]
    m_ref, l_ref, acc_ref, suf_ref, lf_ref = rest[2 * n_pp + 1:]
    b = pl.program_id(0)
    g = pl.program_id(1)
    nrow = n_new * ROWS_PER_TOKEN
    qbd = qbd_ref[0]
    u = u_ref[...]

    lfcol = lfcol_ref[0]
    cnew_col = jnp.concatenate(
        [sum(lfcol[j * ROWS_PER_TOKEN:(j + 1) * ROWS_PER_TOKEN] for j in range(t + 1)) for t in range(n_new)], axis=0)

    def attend(s, pv):
        m_new = jnp.maximum(m_ref[...], jnp.max(s, axis=1, keepdims=True))
        alpha = jnp.exp(m_ref[...] - m_new)
        p = jnp.exp(s - m_new)
        l_ref[...] = alpha * l_ref[...] + jnp.sum(p, axis=1, keepdims=True)
        acc_ref[...] = alpha * acc_ref[...] + pv(p.astype(BF16))
        m_ref[...] = m_new

    @pl.when(g == 0)
    def _():
        m_ref[...] = jnp.full_like(m_ref, NEG_INF)
        l_ref[...] = jnp.zeros_like(l_ref)
        acc_ref[...] = jnp.zeros_like(acc_ref)
        suf_ref[...] = jnp.zeros_like(suf_ref)
        lf_ref[...] = jnp.zeros_like(lf_ref)
        lane = lax.broadcasted_iota(jnp.int32, (nrow, 128), 1)
        lfmat = lfmat_ref[0]
        cnew_mat = jnp.zeros_like(lfmat)
        for j in range(n_new):
            cnew_mat = cnew_mat + jnp.where(lane >= j, lfmat[:, j:j + 1], 0.0)
        s = _dot(qbd, knew_ref[0]) + (cnew_col - cnew_mat)
        s = jnp.where(lane <= _row_token((nrow, 128)), s, NEG_INF)
        attend(s, lambda p: _dot(p, vnew_ref[0]))

    for ii in range(n_pp):
        page = pt_ref[b, n_pages - 1 - (g * n_pp + ii)]
        for h in range(H_FOX):
            lf_ref[pl.ds(ii * ROWS_PER_TOKEN + h, 1), :] = lfc_ref[0, h, pl.ds(page, 1), :]
    lf_all = lf_ref[...]
    within = _dot_acc3(lf_all, u)
    totals = jnp.sum(lf_all, axis=1, keepdims=True)
    later_pages = suf_ref[...]
    scores = []
    for ii in range(n_pp):
        sl = slice(ii * ROWS_PER_TOKEN, (ii + 1) * ROWS_PER_TOKEN)
        bias = jnp.concatenate([within[sl] + later_pages] * n_new, axis=0) + cnew_col
        kp = k_refs[ii][0, 0].reshape(H_FOX * HEAD_DIM, 128).astype(BF16)
        scores.append(_dot(qbd, kp) + bias)
        later_pages = later_pages + totals[sl]
    suf_ref[...] = later_pages

    def pv(p):
        out = None
        for ii in range(n_pp):
            vp = v_refs[ii][0, 0].reshape(H_FOX * HEAD_DIM, 128).astype(BF16)
            part = _dot_nt(p[:, ii * 128:(ii + 1) * 128], vp)
            out = part if out is None else out + part
        return out

    attend(jnp.concatenate(scores, axis=1), pv)

    @pl.when(g == pl.num_programs(1) - 1)
    def _():
        o_ref[0] = _collapse_heads(acc_ref[...] / l_ref[...], n_new, GROUP_W)


def _pages_per_step(n_pages):
    return min(32, n_pages)


def _fox_decode(layer, page_table, qbd, knew_t, vnew, lfcol, lfmat, cache_k, cache_v, cache_lf):
    nb, nrow, w = qbd.shape
    n_new = nrow // ROWS_PER_TOKEN
    n_pages = page_table.shape[1]
    n_pp = _pages_per_step(n_pages)
    u = (jnp.arange(128)[:, None] > jnp.arange(128)[None, :]).astype(BF16)
    per_seq = lambda a: pl.BlockSpec((1,) + a.shape[1:], lambda b, g, pt: (b,) + (0,) * (a.ndim - 1))

    def page_spec(ii):
        return pl.BlockSpec((1, 1) + cache_k.shape[2:],
                            lambda b, g, pt: (layer, pt[b, n_pages - 1 - (g * n_pp + ii)], 0, 0, 0))

    grid_spec = pltpu.PrefetchScalarGridSpec(
        num_scalar_prefetch=1,
        grid=(nb, n_pages // n_pp),
        in_specs=[per_seq(qbd), per_seq(knew_t), per_seq(vnew), per_seq(lfcol), per_seq(lfmat),
                  pl.BlockSpec((1,) + cache_lf.shape[1:], lambda b, g, pt: (layer, 0, 0, 0)),
                  pl.BlockSpec(u.shape, lambda b, g, pt: (0, 0))]
        + [page_spec(ii) for ii in range(n_pp)] * 2,
        out_specs=pl.BlockSpec((1, n_new, w), lambda b, g, pt: (b, 0, 0)),
        scratch_shapes=[pltpu.VMEM((nrow, 1), F32), pltpu.VMEM((nrow, 1), F32), pltpu.VMEM((nrow, w), F32),
                        pltpu.VMEM((ROWS_PER_TOKEN, 1), F32), pltpu.VMEM((n_pp * ROWS_PER_TOKEN, 128), F32)],
    )
    return pl.pallas_call(
        functools.partial(_fox_decode_kernel, n_pp=n_pp, n_new=n_new, n_pages=n_pages),
        grid_spec=grid_spec,
        out_shape=jax.ShapeDtypeStruct((nb, n_new, w), F32),
        compiler_params=_cparams("parallel", "arbitrary"),
        name="fox_decode",
    )(page_table, qbd, knew_t, vnew, lfcol, lfmat, cache_lf, u, *([cache_k] * n_pp), *([cache_v] * n_pp))


def _sb_decode_kernel(pt_ref, qbd_ref, knew_ref, vnew_ref, u_ref, *rest, n_pp, n_new):
    k_refs = rest[:n_pp]
    v_refs = rest[n_pp:2 * n_pp]
    o_ref = rest[2 * n_pp]
    rest_ref, acc_ref = rest[2 * n_pp + 1:]
    g = pl.program_id(1)
    nrow = n_new * ROWS_PER_TOKEN
    width = H_SB * HEAD_DIM
    qbd = qbd_ref[0]
    u = u_ref[...]

    def weights(z_blocks, valid):
        n = len(z_blocks)
        z = jnp.concatenate(z_blocks, axis=0)
        pr = jnp.maximum(z, jnp.log2(1.0 + jnp.exp2(jnp.minimum(z, EXP2_CLAMP))))
        if valid is not None:
            pr = jnp.where(valid, pr, 0.0)
        upto = _dot(pr.astype(BF16), u)
        run = rest_ref[...]
        out = []
        for ii in range(n):
            sl = slice(ii * nrow, (ii + 1) * nrow)
            a = jnp.exp2(z[sl] + upto[sl] + run)
            if valid is not None:
                a = jnp.where(valid, a, 0.0)
            out.append(a.astype(BF16))
            run = run + upto[sl, 0:1]
        rest_ref[...] = run
        return out

    @pl.when(g == 0)
    def _():
        rest_ref[...] = jnp.zeros_like(rest_ref)
        acc_ref[...] = jnp.zeros_like(acc_ref)
        lane = lax.broadcasted_iota(jnp.int32, (nrow, 128), 1)
        (a,) = weights([_dot(qbd, knew_ref[0])], lane < _row_token((nrow, 128)))
        acc_ref[...] = _dot(a, vnew_ref[0])

    z_blocks = [_dot(qbd, k_refs[ii][0, 0].reshape(width, 128).astype(BF16)) for ii in range(n_pp)]
    a_blocks = weights(z_blocks, None)
    out = acc_ref[...]
    for ii in range(n_pp):
        out = out + _dot_nt(a_blocks[ii], v_refs[ii][0, 0].reshape(width, 128).astype(BF16))
    acc_ref[...] = out

    @pl.when(g == pl.num_programs(1) - 1)
    def _():
        o_ref[0] = _collapse_heads(acc_ref[...], n_new, width)


def _sb_decode(layer, page_table, qbd, knew_t, vnew, cache_k, cache_v):
    nb, nrow, width = qbd.shape
    n_new = nrow // ROWS_PER_TOKEN
    n_pages = page_table.shape[1]
    n_pp = _pages_per_step(n_pages)
    u = -(jnp.arange(128)[:, None] >= jnp.arange(128)[None, :]).astype(BF16)
    per_seq = lambda a: pl.BlockSpec((1,) + a.shape[1:], lambda b, g, pt: (b,) + (0,) * (a.ndim - 1))

    def page_spec(ii):
        return pl.BlockSpec((1, 1) + cache_k.shape[2:],
                            lambda b, g, pt: (layer, pt[b, n_pages - 1 - (g * n_pp + ii)], 0, 0, 0))

    grid_spec = pltpu.PrefetchScalarGridSpec(
        num_scalar_prefetch=1,
        grid=(nb, n_pages // n_pp),
        in_specs=[per_seq(qbd), per_seq(knew_t), per_seq(vnew), pl.BlockSpec(u.shape, lambda b, g, pt: (0, 0))]
        + [page_spec(ii) for ii in range(n_pp)] * 2,
        out_specs=pl.BlockSpec((1, n_new, width), lambda b, g, pt: (b, 0, 0)),
        scratch_shapes=[pltpu.VMEM((nrow, 1), F32), pltpu.VMEM((nrow, width), F32)],
    )
    return pl.pallas_call(
        functools.partial(_sb_decode_kernel, n_pp=n_pp, n_new=n_new),
        grid_spec=grid_spec,
        out_shape=jax.ShapeDtypeStruct((nb, n_new, width), F32),
        compiler_params=_cparams("parallel", "arbitrary"),
        name="sb_decode",
    )(page_table, qbd, knew_t, vnew, u, *([cache_k] * n_pp), *([cache_v] * n_pp))


def _pad_cols(a, width=GROUP_W):
    return jnp.pad(a, ((0, 0), (0, width - a.shape[1])))


def _split_mix_weight(w):
    wf, wr, ws = H_FOX * HEAD_DIM, H_RET * HEAD_DIM, H_SB * HEAD_DIM
    sizes = (wf, wf, wf, H_FOX, wr, wr, wr, wr, ws, ws, ws)
    offs = np.cumsum((0,) + sizes)
    return [w[:, offs[i]:offs[i + 1]] for i in range(len(sizes))]


def _rope_swap_perm(n_heads):
    idx = np.arange(n_heads * HEAD_DIM)
    return (idx // HEAD_DIM) * HEAD_DIM + (idx % HEAD_DIM + HEAD_DIM // 2) % HEAD_DIM


def _rope_tables(pos, n_heads):
    half = HEAD_DIM // 2
    inv_freq = ROPE_BASE ** (-jnp.arange(half, dtype=F32) / half)
    ang = pos.astype(F32)[:, None] * inv_freq[None, :]
    cos, sin = jnp.cos(ang), jnp.sin(ang)
    cos_t = jnp.tile(jnp.concatenate([cos, cos], axis=1), (1, n_heads))
    sin_t = jnp.tile(jnp.concatenate([-sin, sin], axis=1), (1, n_heads))
    return _pad_cols(cos_t), _pad_cols(sin_t)


def _block_diag_queries(q, n_heads):
    nb, n_new, w = q.shape
    slot = jnp.arange(ROWS_PER_TOKEN)[:, None]
    lane_head = (jnp.arange(w) // HEAD_DIM)[None, :]
    mask = (slot == lane_head) & (slot < n_heads)
    qb = jnp.where(mask[None, None], q[:, :, None, :], 0.0)
    return qb.reshape(nb, n_new * ROWS_PER_TOKEN, w).astype(BF16)


def _new_keys_t(k, width):
    nb, n_new, _ = k.shape
    kt = jnp.swapaxes(k[:, :, :width], 1, 2)
    return jnp.pad(kt, ((0, 0), (0, 0), (0, 128 - n_new))).astype(BF16)


def _new_values(v, width):
    nb, n_new, _ = v.shape
    return jnp.pad(v[:, :, :width], ((0, 0), (0, 128 - n_new), (0, 0))).astype(BF16)


def kernel(x_prompt, x_sample, cache_fox_k, cache_fox_v, cache_fox_logf, cache_sb_k, cache_sb_v, state_ret,
           page_table, norm_g, w_ffn1_in, w_ffn1_out, w_mix_in, b_forget, ret_gn_g, w_mix_out, w_ffn2_in,
           w_ffn2_out):
    depth = norm_g.shape[0]
    bsz, seq, d = x_prompt.shape
    nb, n_new, _ = x_sample.shape
    n_pages, page_size = page_table.shape[1], cache_fox_k.shape[2]
    past = n_pages * page_size
    wf, wr, ws = H_FOX * HEAD_DIM, H_RET * HEAD_DIM, H_SB * HEAD_DIM
    w = GROUP_W

    ck_fox = jnp.transpose(cache_fox_k, (0, 1, 3, 4, 2))
    cv_fox = jnp.transpose(cache_fox_v, (0, 1, 3, 4, 2))
    ck_sb = jnp.transpose(cache_sb_k, (0, 1, 3, 4, 2))
    cv_sb = jnp.transpose(cache_sb_v, (0, 1, 3, 4, 2))
    clf = jnp.transpose(cache_fox_logf, (0, 3, 1, 2))

    cos_p, sin_p = _rope_tables(jnp.arange(seq, dtype=jnp.int32), H_RET)
    cos_s, sin_s = _rope_tables(past + jnp.arange(n_new, dtype=jnp.int32), H_RET)
    cos_s, sin_s = jnp.tile(cos_s, (nb, 1)), jnp.tile(sin_s, (nb, 1))
    perm = _rope_swap_perm(H_RET)
    avg = jnp.asarray(np.kron(np.eye(w // HEAD_DIM), np.full((HEAD_DIM, HEAD_DIM), 1.0 / HEAD_DIM)), BF16)

    w1_in, w1_out = w_ffn1_in.astype(BF16), w_ffn1_out.astype(BF16)
    w2_in, w2_out = w_ffn2_in.astype(BF16), w_ffn2_out.astype(BF16)
    yp = x_prompt.reshape(bsz * seq, d)
    ys = x_sample.reshape(nb * n_new, d)
    st_p, st_s = [], []
    kv_stacked = None
    for l in range(depth):
        g = [norm_g[l, i][None, :] for i in range(norm_g.shape[1])]
        qa, ka, va, fa, qr, kr, vr, gr, qs, ks, vs = _split_mix_weight(w_mix_in[l])
        ret_cols = [_pad_cols(qr), _pad_cols(qr[:, perm]), _pad_cols(kr), _pad_cols(kr[:, perm]),
                    _pad_cols(vr), _pad_cols(gr)]
        kv_cols = [ka, va, _pad_cols(ks), _pad_cols(vs)]
        wrow = jnp.concatenate([ka, _pad_cols(ks)] + ret_cols, axis=1).astype(BF16)
        wt = jnp.concatenate(kv_cols + [qa, _pad_cols(qs), _pad_cols(fa, 16)], axis=1).T.astype(BF16)
        w_samp = jnp.concatenate([qa, _pad_cols(qs)] + ret_cols + kv_cols + [_pad_cols(fa, 128)], axis=1).astype(BF16)
        bf_col = jnp.pad(b_forget[l], (0, 16 - H_FOX))[:, None]
        bf_row = jnp.pad(b_forget[l], (0, 128 - H_FOX))[None, :]
        gn = _pad_cols(ret_gn_g[l][None, :])
        wo = w_mix_out[l]
        wo1 = wo[:wf].astype(BF16)
        wo2 = jnp.pad(wo[wf:wf + wr], ((0, w - wr), (0, 0))).astype(BF16)
        wo3 = jnp.pad(wo[wf + wr:], ((0, w - ws), (0, 0))).astype(BF16)

        hp = _ffn_block(yp, g[0], w1_in, w1_out, g[1], l)
        (ksr, qr_p, kr_p, vr_p, gr_p, kaug, kat, vat, kst, vst, vstb, qat, qst, lft, qaug, vtaug) = _proj_prompt(
            hp.reshape(bsz, seq, d), g[2], wrow, wt, bf_col, cos_p, sin_p, l, depth, kv_stacked)
        kv_stacked = (kat, vat, kst, vst)
        o_fox = _fox_prompt(kaug, qat, qaug, vtaug)
        o_ret, s_raw = _ret_prompt(qr_p, kr_p, vr_p)
        o_sb_a, o_sb_b = _sb_prompt(ksr, qst, vstb)
        yp = _merge_ffn(hp, o_fox.reshape(bsz * seq, w), o_ret.reshape(bsz * seq, w), gr_p.reshape(bsz * seq, w),
                        o_sb_a.reshape(bsz * seq, -1), o_sb_b.reshape(bsz * seq, -1), gn, avg, wo1, wo2, wo3, g[3],
                        g[4], w2_in, w2_out, g[5], l)
        s_fin = jnp.stack([s_raw[:, h // 2, h % 2, (h % 2) * HEAD_DIM:(h % 2 + 1) * HEAD_DIM,
                                 (h % 2) * HEAD_DIM:(h % 2 + 1) * HEAD_DIM] for h in range(H_RET)], axis=1)
        st_p.append((jnp.transpose(lft[:, :H_FOX], (0, 2, 1)), s_fin))

        hs = _ffn_block(ys, g[0], w1_in, w1_out, g[1], l)
        (qa_s, qs_s, qr_s, kr_s, vr_s, gr_s, ka_s, va_s, ks_s, vs_s, lf_s) = _proj_sample(
            hs, g[2], w_samp, bf_row, cos_s, sin_s)
        seqs = lambda a: a.reshape(nb, n_new, a.shape[-1])
        lf_new = seqs(lf_s)[:, :, :ROWS_PER_TOKEN]
        lfcol = lf_new.reshape(nb, n_new * ROWS_PER_TOKEN, 1)
        lfmat = jnp.broadcast_to(jnp.transpose(lf_new, (0, 2, 1))[:, None], (nb, n_new, ROWS_PER_TOKEN, n_new))
        lfmat = jnp.pad(lfmat.reshape(nb, n_new * ROWS_PER_TOKEN, n_new), ((0, 0), (0, 0), (0, 128 - n_new)))
        o_fox_s = _fox_decode(l, page_table, _block_diag_queries(seqs(qa_s), H_FOX), _new_keys_t(seqs(ka_s), wf),
                              _new_values(seqs(va_s), wf), lfcol, lfmat, ck_fox, cv_fox, clf)
        o_sb_s = _sb_decode(l, page_table, _block_diag_queries(seqs(qs_s)[:, :, :ws] * LOG2E, H_SB),
                            _new_keys_t(seqs(ks_s), ws), _new_values(seqs(vs_s), ws), ck_sb, cv_sb)
        o_ret_s, s_new = _ret_sample(seqs(qr_s), seqs(kr_s), seqs(vr_s), state_ret[l])
        o_sb_rows = _pad_cols(o_sb_s.reshape(nb * n_new, ws)).astype(BF16)
        ys = _merge_ffn(hs, o_fox_s.reshape(nb * n_new, w).astype(BF16), o_ret_s.reshape(nb * n_new, w), gr_s,
                        o_sb_rows[:, :o_sb_a.shape[-1]], o_sb_rows[:, o_sb_a.shape[-1]:], gn, avg, wo1, wo2, wo3, g[3],
                        g[4], w2_in, w2_out, g[5], l)
        heads = lambda a, nh: a[:, :nh * HEAD_DIM].reshape(nb, n_new, nh, HEAD_DIM)
        st_s.append((heads(ka_s, H_FOX), heads(va_s, H_FOX), seqs(lf_s)[:, :, :H_FOX],
                     heads(ks_s, H_SB), heads(vs_s, H_SB), s_new))

    stk = lambda sts, i: jnp.stack([s[i] for s in sts], axis=0)
    to_heads = lambda a, nh: jnp.transpose(a.reshape(depth, bsz, nh, HEAD_DIM, seq), (0, 1, 4, 2, 3))
    kat, vat, kst, vst = kv_stacked
    return (yp.reshape(bsz, seq, d), ys.reshape(nb, n_new, d),
            to_heads(kat, H_FOX), to_heads(vat, H_FOX), stk(st_p, 0), to_heads(kst, H_SB), to_heads(vst, H_SB), stk(st_p, 1),
            stk(st_s, 0), stk(st_s, 1), stk(st_s, 2), stk(st_s, 3), stk(st_s, 4), stk(st_s, 5))
```
